```python
import math
import jax, jax.numpy as jnp
from jax import lax
import numpy as np

D_MODEL = 4096
BATCH = 1
SEQ = 8192
DEPTH = 1
DEC_BATCH = 32
DEC_SEQ = 4
PAST_LEN = 8192
PAGE_SIZE = 128

N_META = 16
MIX_WIDTH = D_MODEL
M_WIDTH = MIX_WIDTH // 2
M_HEADS = 8
M_DV = M_WIDTH // M_HEADS
M_DK = M_DV // 2
D_WIDTH = MIX_WIDTH - M_WIDTH
DIFF_DV = 128
DIFF_HEADS = D_WIDTH // DIFF_DV
DIFF_DH = DIFF_DV // 2
D_FF = ((8 * D_MODEL // 3 + 255) // 256) * 256
CHUNK = 64
Q_BLOCK = 128
ROPE_THETA = 10000.0
GATE_CAP = 15.0
M_INIT = -1e30
EPS = 1e-6
DIFF_SCALE = DIFF_DH ** -0.5
IN_SIZES = (M_HEADS * M_DK, M_HEADS * M_DK, M_WIDTH, M_WIDTH, M_HEADS, M_HEADS,
            DIFF_HEADS * 2 * DIFF_DH, DIFF_HEADS * 2 * DIFF_DH, D_WIDTH)
IN_COLS = sum(IN_SIZES)
IN_SPLITS = tuple(sum(IN_SIZES[:i + 1]) for i in range(len(IN_SIZES) - 1))

kernel_name = 'hymba_mlstm_diffattn_macaron_step'

F32 = jnp.float32


def rmsnorm(x, g):
    xf = x.astype(F32)
    y = xf * lax.rsqrt(jnp.mean(xf * xf, axis=-1, keepdims=True) + EPS)
    return (y * g.astype(F32)).astype(x.dtype)


def rms_f32(x):
    return x * lax.rsqrt(jnp.mean(x * x, axis=-1, keepdims=True) + EPS)


def swiglu(x, wg, wu, wd):
    return (jax.nn.silu(x @ wg) * (x @ wu)) @ wd


def soft_cap(x):
    return GATE_CAP * jnp.tanh(x.astype(F32) / GATE_CAP)


def rope(x, pos):
    half = DIFF_DH // 2
    inv = ROPE_THETA ** (-jnp.arange(half, dtype=F32) / half)
    ang = pos.astype(F32)[:, None] * inv[None, :]
    cos = jnp.cos(ang)[:, None, None, :]
    sin = jnp.sin(ang)[:, None, None, :]
    xf = x.astype(F32)
    x1, x2 = xf[..., :half], xf[..., half:]
    return jnp.concatenate([x1 * cos - x2 * sin, x1 * sin + x2 * cos], axis=-1).astype(x.dtype)


def project_mixers(u, w_in, b_i, b_f, pos):
    B, T, _ = u.shape
    z = u @ w_in
    q_m, k_m, v_m, o_m, i_m, f_m, q_d, k_d, v_d = jnp.split(z, IN_SPLITS, axis=-1)
    heads = lambda a: a.reshape(B, T, M_HEADS, -1).transpose(0, 2, 1, 3).astype(F32)
    q_m = heads(q_m)
    k_m = heads(k_m) * (M_DK ** -0.5)
    v_m = heads(v_m)
    log_i = soft_cap(i_m + b_i).transpose(0, 2, 1)
    log_f = jax.nn.log_sigmoid(soft_cap(f_m + b_f)).transpose(0, 2, 1)
    q_d = rope(q_d.reshape(B, T, DIFF_HEADS, 2, DIFF_DH), pos)
    k_d = rope(k_d.reshape(B, T, DIFF_HEADS, 2, DIFF_DH), pos)
    v_d = v_d.reshape(B, T, DIFF_HEADS, DIFF_DV)
    return (q_m, k_m, v_m, log_i, log_f), o_m, (q_d, k_d, v_d)


def mlstm_chunk(C, n, m, q, k, v, log_i, log_f):
    C, n, m = C.astype(F32), n.astype(F32), m.astype(F32)
    T = q.shape[2]
    b = jnp.cumsum(log_f, axis=-1)
    causal = jnp.tril(jnp.ones((T, T), bool))
    log_d = jnp.where(causal, b[..., :, None] - b[..., None, :] + log_i[..., None, :], -jnp.inf)
    m_inter = m[..., None] + b
    m_t = jnp.maximum(m_inter, jnp.max(log_d, axis=-1))
    w_intra = jnp.exp(log_d - m_t[..., None])
    w_inter = jnp.exp(m_inter - m_t)
    s = jnp.einsum('bhtd,bhsd->bhts', q, k) * w_intra
    num = w_inter[..., None] * jnp.einsum('bhtd,bhde->bhte', q, C) + jnp.einsum('bhts,bhse->bhte', s, v)
    den = w_inter * jnp.einsum('bhtd,bhd->bht', q, n) + jnp.sum(s, axis=-1)
    h = num / jnp.maximum(jnp.abs(den), jnp.exp(-m_t))[..., None]
    m_new = m_t[..., -1]
    decay = jnp.exp(m_inter[..., -1] - m_new)
    w_k = jnp.exp(b[..., -1:] - b + log_i - m_new[..., None])
    C_new = decay[..., None, None] * C + jnp.einsum('bhs,bhsd,bhse->bhde', w_k, k, v)
    n_new = decay[..., None] * n + jnp.einsum('bhs,bhsd->bhd', w_k, k)
    return C_new, n_new, m_new, h


def mlstm_prompt(q, k, v, log_i, log_f):
    B = q.shape[0]
    n_chunks = (q.shape[2] - N_META) // CHUNK
    C0 = jnp.zeros((B, M_HEADS, M_DK, M_DV), F32)
    n0 = jnp.zeros((B, M_HEADS, M_DK), F32)
    m0 = jnp.full((B, M_HEADS), M_INIT, F32)
    C, n, m, h_meta = mlstm_chunk(C0, n0, m0, q[:, :, :N_META], k[:, :, :N_META], v[:, :, :N_META],
                                  log_i[:, :, :N_META], log_f[:, :, :N_META])

    def to_chunks(a):
        a = a[:, :, N_META:]
        a = a.reshape(a.shape[:2] + (n_chunks, CHUNK) + a.shape[3:])
        return jnp.moveaxis(a, 2, 0)

    def step(carry, xs):
        Cc, nc, mc = carry
        Cc, nc, mc, h = mlstm_chunk(Cc, nc, mc, *xs)
        return (Cc, nc, mc), h

    (C, n, m), h_real = lax.scan(step, (C, n, m), (to_chunks(q), to_chunks(k), to_chunks(v),
                                                   to_chunks(log_i), to_chunks(log_f)))
    h_real = jnp.moveaxis(h_real, 0, 2).reshape(B, M_HEADS, n_chunks * CHUNK, M_DV)
    return jnp.concatenate([h_meta, h_real], axis=2), C, n, m


def diff_lambda(lam_p, lam_init):
    lp = lam_p.astype(F32)
    return jnp.exp(jnp.sum(lp[0] * lp[1])) - jnp.exp(jnp.sum(lp[2] * lp[3])) + lam_init


def diff_attend(q, k, v, mask, lam):
    s = jnp.einsum('bqhcd,bkhcd->bchqk', q.astype(F32), k.astype(F32)) * DIFF_SCALE
    s = jnp.where(mask, s, -jnp.inf)
    p = jax.nn.softmax(s, axis=-1)
    a = p[:, 0] - lam * p[:, 1]
    return jnp.einsum('bhqk,bkhd->bqhd', a, v.astype(F32))


def diff_attn_prompt(q, k, v, lam):
    B, L = q.shape[:2]
    n_blocks = (L - N_META) // Q_BLOCK
    o_meta = diff_attend(q[:, :N_META], k[:, :N_META], v[:, :N_META],
                         jnp.tril(jnp.ones((N_META, N_META), bool)), lam)
    q_real = q[:, N_META:]
    kpos = jnp.arange(L)

    def block(j):
        qb = lax.dynamic_slice_in_dim(q_real, j * Q_BLOCK, Q_BLOCK, axis=1)
        qpos = N_META + j * Q_BLOCK + jnp.arange(Q_BLOCK)
        return diff_attend(qb, k, v, kpos[None, :] <= qpos[:, None], lam)

    o_real = lax.map(block, jnp.arange(n_blocks))
    o_real = jnp.moveaxis(o_real, 0, 1).reshape(B, n_blocks * Q_BLOCK, DIFF_HEADS, DIFF_DV)
    return jnp.concatenate([o_meta, o_real], axis=1)


def diff_attn_sample(cache_k, cache_v, page_table, q, k_new, v_new, lam):
    Tn = q.shape[1]
    past = page_table.shape[1] * cache_k.shape[1]
    kpos = jnp.arange(past + Tn)
    qpos = past + jnp.arange(Tn)
    mask = kpos[None, :] <= qpos[:, None]

    def one(args):
        pt, qi, ki, vi = args
        k_all = jnp.concatenate([cache_k[pt].reshape((-1,) + cache_k.shape[2:]).astype(ki.dtype), ki], axis=0)
        v_all = jnp.concatenate([cache_v[pt].reshape((-1,) + cache_v.shape[2:]).astype(vi.dtype), vi], axis=0)
        return diff_attend(qi[None], k_all[None], v_all[None], mask, lam)[0]

    return lax.map(one, (page_table, q, k_new, v_new))


def merge_mixers(h_m, o_m, m_norm, o_d, d_subln, lam_init, w_out):
    B, H, T, _ = h_m.shape
    hm = rms_f32(h_m) * m_norm.astype(F32)[None, :, None, :]
    hm = hm.transpose(0, 2, 1, 3).reshape(B, T, M_WIDTH)
    m_out = jax.nn.sigmoid(o_m.astype(F32)) * hm
    d_out = (rms_f32(o_d) * d_subln.astype(F32) * (1.0 - lam_init)).reshape(B, T, D_WIDTH)
    mix = jnp.concatenate([m_out, d_out], axis=-1).astype(o_m.dtype)
    return mix @ w_out


def setup_inputs(seed: int = 0) -> dict:
    key = jax.random.key(seed)
    ks = jax.random.split(key, 32)
    nrm = lambda k, s, sc: jax.random.normal(k, s, F32) * sc
    gain = lambda k, s: 1.0 + 0.02 * jax.random.normal(k, s, F32)
    n_pages = PAST_LEN // PAGE_SIZE
    n_used = DEC_BATCH * n_pages
    n_phys = (5 * n_used) // 4
    page_table = jax.random.permutation(ks[7], n_phys)[:n_used].reshape(DEC_BATCH, n_pages).astype(jnp.int32)
    return {
        'x_prompt': nrm(ks[0], (BATCH, SEQ, D_MODEL), 1.0),
        'x_sample': nrm(ks[1], (DEC_BATCH, DEC_SEQ, D_MODEL), 1.0),
        'cache_k': nrm(ks[2], (DEPTH, n_phys, PAGE_SIZE, DIFF_HEADS, 2, DIFF_DH), 1.0),
        'cache_v': nrm(ks[3], (DEPTH, n_phys, PAGE_SIZE, DIFF_HEADS, DIFF_DV), 1.0),
        'state_C': nrm(ks[4], (DEPTH, DEC_BATCH, M_HEADS, M_DK, M_DV), 0.5),
        'state_n': nrm(ks[5], (DEPTH, DEC_BATCH, M_HEADS, M_DK), 0.5),
        'state_m': nrm(ks[6], (DEPTH, DEC_BATCH, M_HEADS), 1.0),
        'page_table': page_table,
        'meta_tokens': nrm(ks[8], (N_META, D_MODEL), 1.0),
        'norm_ffn1': gain(ks[9], (DEPTH, D_MODEL)),
        'ffn1_w_gate': nrm(ks[10], (DEPTH, D_MODEL, D_FF), D_MODEL ** -0.5),
        'ffn1_w_up': nrm(ks[11], (DEPTH, D_MODEL, D_FF), D_MODEL ** -0.5),
        'ffn1_w_down': nrm(ks[12], (DEPTH, D_FF, D_MODEL), D_FF ** -0.5),
        'norm_mix': gain(ks[13], (DEPTH, D_MODEL)),
        'w_in': nrm(ks[14], (DEPTH, D_MODEL, IN_COLS), D_MODEL ** -0.5),
        'mlstm_b_i': nrm(ks[15], (DEPTH, M_HEADS), 0.1),
        'mlstm_b_f': 3.0 + nrm(ks[16], (DEPTH, M_HEADS), 0.5),
        'mlstm_norm': gain(ks[17], (DEPTH, M_HEADS, M_DV)),
        'diff_lambda_qk': nrm(ks[18], (DEPTH, 4, DIFF_DH), 0.1),
        'diff_subln': gain(ks[19], (DEPTH, DIFF_DV)),
        'w_out': nrm(ks[20], (DEPTH, MIX_WIDTH, D_MODEL), MIX_WIDTH ** -0.5),
        'norm_ffn2': gain(ks[21], (DEPTH, D_MODEL)),
        'ffn2_w_gate': nrm(ks[22], (DEPTH, D_MODEL, D_FF), D_MODEL ** -0.5),
        'ffn2_w_up': nrm(ks[23], (DEPTH, D_MODEL, D_FF), D_MODEL ** -0.5),
        'ffn2_w_down': nrm(ks[24], (DEPTH, D_FF, D_MODEL), D_FF ** -0.5),
        'norm_final': gain(ks[25], (D_MODEL,)),
    }


def reference(x_prompt, x_sample, cache_k, cache_v, state_C, state_n, state_m, page_table, meta_tokens,
              norm_ffn1, ffn1_w_gate, ffn1_w_up, ffn1_w_down, norm_mix, w_in, mlstm_b_i, mlstm_b_f,
              mlstm_norm, diff_lambda_qk, diff_subln, w_out, norm_ffn2, ffn2_w_gate, ffn2_w_up,
              ffn2_w_down, norm_final):
    B = x_prompt.shape[0]
    meta = jnp.broadcast_to(meta_tokens[None].astype(x_prompt.dtype), (B, N_META, D_MODEL))
    hp = jnp.concatenate([meta, x_prompt], axis=1)
    hs = x_sample
    pos_p = jnp.arange(hp.shape[1])
    pos_s = PAST_LEN + jnp.arange(hs.shape[1])
    kp, vp, Cp, np_, mp_, ksm, vsm, Csm, nsm, msm = ([] for _ in range(10))
    for l in range(DEPTH):
        lam_init = 0.8 - 0.6 * math.exp(-0.3 * l)
        lam = diff_lambda(diff_lambda_qk[l], lam_init)
        hp = hp + 0.5 * swiglu(rmsnorm(hp, norm_ffn1[l]), ffn1_w_gate[l], ffn1_w_up[l], ffn1_w_down[l])
        hs = hs + 0.5 * swiglu(rmsnorm(hs, norm_ffn1[l]), ffn1_w_gate[l], ffn1_w_up[l], ffn1_w_down[l])
        m_in_p, o_m_p, d_in_p = project_mixers(rmsnorm(hp, norm_mix[l]), w_in[l], mlstm_b_i[l], mlstm_b_f[l], pos_p)
        m_in_s, o_m_s, d_in_s = project_mixers(rmsnorm(hs, norm_mix[l]), w_in[l], mlstm_b_i[l], mlstm_b_f[l], pos_s)
        h_mp, C_p, n_p, m_p = mlstm_prompt(*m_in_p)
        C_s, n_s, m_s, h_ms = mlstm_chunk(state_C[l], state_n[l], state_m[l], *m_in_s)
        o_dp = diff_attn_prompt(*d_in_p, lam)
        o_ds = diff_attn_sample(cache_k[l], cache_v[l], page_table, *d_in_s, lam)
        hp = hp + merge_mixers(h_mp, o_m_p, mlstm_norm[l], o_dp, diff_subln[l], lam_init, w_out[l])
        hs = hs + merge_mixers(h_ms, o_m_s, mlstm_norm[l], o_ds, diff_subln[l], lam_init, w_out[l])
        hp = hp + 0.5 * swiglu(rmsnorm(hp, norm_ffn2[l]), ffn2_w_gate[l], ffn2_w_up[l], ffn2_w_down[l])
        hs = hs + 0.5 * swiglu(rmsnorm(hs, norm_ffn2[l]), ffn2_w_gate[l], ffn2_w_up[l], ffn2_w_down[l])
        kp.append(d_in_p[1]); vp.append(d_in_p[2]); Cp.append(C_p); np_.append(n_p); mp_.append(m_p)
        ksm.append(d_in_s[1]); vsm.append(d_in_s[2]); Csm.append(C_s); nsm.append(n_s); msm.append(m_s)
    y_prompt = rmsnorm(hp[:, N_META:], norm_final)
    y_sample = rmsnorm(hs, norm_final)
    return (y_prompt, y_sample,
            jnp.stack(kp), jnp.stack(vp), jnp.stack(Cp), jnp.stack(np_), jnp.stack(mp_),
            jnp.stack(ksm), jnp.stack(vsm), jnp.stack(Csm), jnp.stack(nsm), jnp.stack(msm))
```

```python
import functools
import math

import numpy as np
import jax
import jax.numpy as jnp
from jax import lax
from jax.experimental import pallas as pl
from jax.experimental.pallas import tpu as pltpu

F32 = jnp.float32
BF16 = jnp.bfloat16

N_META = 16
CHUNK = 64
ROPE_THETA = 10000.0
GATE_CAP = 15.0
M_INIT = -1e30
EPS = 1e-6
DIFF_DV = 128
DIFF_DH = DIFF_DV // 2
DIFF_SCALE = DIFF_DH ** -0.5
PAGES_PER_STEP = 4
MASK_VALUE = -1e30

LANES = 128
V7X_VMEM_BYTES = 64 * 2 ** 20
VMEM_LIMIT = V7X_VMEM_BYTES - 8 * 2 ** 20


def _params(*sem):
    return pltpu.CompilerParams(dimension_semantics=sem, vmem_limit_bytes=VMEM_LIMIT)


def _round_up(x, m):
    return (x + m - 1) // m * m


def _row_tile(rows, target):
    n = pl.cdiv(rows, target)
    return _round_up(pl.cdiv(rows, n), 16)


def _dot(a, b):
    return jnp.dot(a.astype(BF16), b.astype(BF16), preferred_element_type=F32)


def _dot_nt(a, b):
    return lax.dot_general(a.astype(BF16), b.astype(BF16), (((1,), (1,)), ((), ())),
                           preferred_element_type=F32)


def _dot_tn(a, b):
    return lax.dot_general(a.astype(BF16), b.astype(BF16), (((0,), (0,)), ((), ())),
                           preferred_element_type=F32)


def _rmsnorm_kernel(x_ref, g_ref, o_ref):
    x = x_ref[...]
    y = x * lax.rsqrt(jnp.mean(x * x, axis=-1, keepdims=True) + EPS)
    o_ref[...] = (y * g_ref[...]).astype(o_ref.dtype)


def _rmsnorm(x, g, out_dtype, row_block, first_block=0, n_blocks=None):
    rows, d = x.shape
    if n_blocks is None:
        n_blocks, out_rows = pl.cdiv(rows, row_block), rows
    else:
        out_rows = n_blocks * row_block
    return pl.pallas_call(
        _rmsnorm_kernel,
        grid=(n_blocks,),
        in_specs=[pl.BlockSpec((row_block, d), lambda i: (i + first_block, 0)),
                  pl.BlockSpec((1, d), lambda i: (0, 0))],
        out_specs=pl.BlockSpec((row_block, d), lambda i: (i, 0)),
        out_shape=jax.ShapeDtypeStruct((out_rows, d), out_dtype),
        compiler_params=_params("parallel"),
        name="rmsnorm",
    )(x, g.reshape(1, d).astype(F32))


def _gateup_kernel(x_ref, wg_ref, wu_ref, o_ref):
    x = x_ref[...]
    g = jnp.dot(x, wg_ref[...], preferred_element_type=F32)
    u = jnp.dot(x, wu_ref[...], preferred_element_type=F32)
    o_ref[...] = (g * jax.nn.sigmoid(g) * u).astype(o_ref.dtype)


def _gateup(x, wg, wu, tm, tn):
    rows, d = x.shape
    n = wg.shape[1]
    return pl.pallas_call(
        _gateup_kernel,
        grid=(pl.cdiv(rows, tm), n // tn),
        in_specs=[pl.BlockSpec((tm, d), lambda i, j: (i, 0)),
                  pl.BlockSpec((d, tn), lambda i, j: (0, j)),
                  pl.BlockSpec((d, tn), lambda i, j: (0, j))],
        out_specs=pl.BlockSpec((tm, tn), lambda i, j: (i, j)),
        out_shape=jax.ShapeDtypeStruct((rows, n), BF16),
        compiler_params=_params("parallel", "arbitrary"),
        name="ffn_gate_up",
    )(x, wg, wu)


def _mm_resid_kernel(a_ref, w_ref, r_ref, o_ref, *, scale):
    acc = jnp.dot(a_ref[...], w_ref[...], preferred_element_type=F32)
    o_ref[...] = r_ref[...] + scale * acc


def _mm_resid(a, w, resid, scale, tm, tn):
    rows, k = a.shape
    n = w.shape[1]
    return pl.pallas_call(
        functools.partial(_mm_resid_kernel, scale=scale),
        grid=(pl.cdiv(rows, tm), n // tn),
        in_specs=[pl.BlockSpec((tm, k), lambda i, j: (i, 0)),
                  pl.BlockSpec((k, tn), lambda i, j: (0, j)),
                  pl.BlockSpec((tm, tn), lambda i, j: (i, j))],
        out_specs=pl.BlockSpec((tm, tn), lambda i, j: (i, j)),
        out_shape=jax.ShapeDtypeStruct((rows, n), F32),
        compiler_params=_params("parallel", "arbitrary"),
        name="matmul_residual",
    )(a, w, resid)


def _mm_kernel(a_ref, w_ref, o_ref):
    o_ref[...] = jnp.dot(a_ref[...], w_ref[...], preferred_element_type=F32)


def _mm(a, w, tm, tn):
    rows, k = a.shape
    n = w.shape[1]
    return pl.pallas_call(
        _mm_kernel,
        grid=(pl.cdiv(rows, tm), n // tn),
        in_specs=[pl.BlockSpec((tm, k), lambda i, j: (i, 0)),
                  pl.BlockSpec((k, tn), lambda i, j: (0, j))],
        out_specs=pl.BlockSpec((tm, tn), lambda i, j: (i, j)),
        out_shape=jax.ShapeDtypeStruct((rows, n), F32),
        compiler_params=_params("parallel", "arbitrary"),
        name="matmul",
    )(a, w)


def _ffn(h, g, wg, wu, wd):
    rows = h.shape[0]
    xn = _rmsnorm(h, g, BF16, 256)
    hid = _gateup(xn, wg, wu, _row_tile(rows, 1088), 512)
    return _mm_resid(hid, wd, h, 0.5, _row_tile(rows, 544), 256)


def _gate_kernel(z_ref, b_ref, o_ref, *, heads):
    z = z_ref[...] + b_ref[...]
    c = GATE_CAP * jnp.tanh(z / GATE_CAP)
    log_sig = jnp.minimum(c, 0.0) - jnp.log1p(jnp.exp(-jnp.abs(c)))
    lane = lax.broadcasted_iota(jnp.int32, c.shape, 1)
    o_ref[...] = jnp.where(lane < heads, c, log_sig)


def _gates(z, bias, heads):
    rows, d = z.shape
    rb = 512
    return pl.pallas_call(
        functools.partial(_gate_kernel, heads=heads),
        grid=(pl.cdiv(rows, rb),),
        in_specs=[pl.BlockSpec((rb, d), lambda i: (i, 0)), pl.BlockSpec((1, d), lambda i: (0, 0))],
        out_specs=pl.BlockSpec((rb, d), lambda i: (i, 0)),
        out_shape=jax.ShapeDtypeStruct((rows, d), F32),
        compiler_params=_params("parallel"),
        name="mlstm_gates",
    )(z, bias)


def _rope_kernel(q_ref, k_ref, v_ref, cos_ref, sin_ref, q0_ref, q1_ref, kf_ref, kb_ref, vb_ref):
    cos = cos_ref[...]
    sin = sin_ref[...]
    lane = lax.broadcasted_iota(jnp.int32, cos.shape, 1)
    first_half = (lane % DIFF_DH) < (DIFF_DH // 2)
    map0 = lane < DIFF_DH

    def rot(x):
        partner = jnp.where(first_half, pltpu.roll(x, LANES - DIFF_DH // 2, 1), pltpu.roll(x, DIFF_DH // 2, 1))
        return x * cos + partner * sin

    for j in range(q_ref.shape[1] // LANES):
        sl = slice(j * LANES, (j + 1) * LANES)
        q = rot(q_ref[:, sl]) * DIFF_SCALE
        q0_ref[:, sl] = jnp.where(map0, q, 0.0).astype(BF16)
        q1_ref[:, sl] = jnp.where(map0, 0.0, q).astype(BF16)
        k = rot(k_ref[:, sl])
        kf_ref[:, sl] = k
        kb_ref[:, sl] = k.astype(BF16)
    vb_ref[...] = v_ref[...].astype(BF16)


def _rope(z, col_q, col_k, col_v, width, cos, sin):
    rows = z.shape[0]
    rb = 256
    zspec = lambda c: pl.BlockSpec((rb, width), lambda i, c=c: (i, c))
    tspec = pl.BlockSpec((rb, LANES), lambda i: (i, 0))
    ospec = pl.BlockSpec((rb, width), lambda i: (i, 0))
    bshape = jax.ShapeDtypeStruct((rows, width), BF16)
    return pl.pallas_call(
        _rope_kernel,
        grid=(pl.cdiv(rows, rb),),
        in_specs=[zspec(col_q), zspec(col_k), zspec(col_v), tspec, tspec],
        out_specs=[ospec] * 5,
        out_shape=[bshape, bshape, jax.ShapeDtypeStruct((rows, width), F32), bshape, bshape],
        compiler_params=_params("parallel"),
        name="rope",
    )(z, z, z, cos, sin)


def _lane_cumsum(x):
    lane = lax.broadcasted_iota(jnp.int32, x.shape, 1)
    k = 1
    while k < x.shape[1]:
        x = x + jnp.where(lane >= k, pltpu.roll(x, k, 1), 0.0)
        k *= 2
    return x


def _mlstm_chunk(q, k, v, li_row, b_row, c_state, n_row, m_prev, valid):
    t = q.shape[0]
    ri = lax.broadcasted_iota(jnp.int32, (t, t), 0)
    ci = lax.broadcasted_iota(jnp.int32, (t, t), 1)
    eye = ri == ci
    b_mat = jnp.broadcast_to(b_row, (t, t))
    b_col = jnp.sum(jnp.where(eye, b_mat, 0.0), axis=1, keepdims=True)
    log_d = jnp.where(ci <= ri, b_col - b_mat + li_row, -jnp.inf)
    m_inter = m_prev + b_col
    m_t = jnp.maximum(m_inter, jnp.max(log_d, axis=1, keepdims=True))
    w_intra = jnp.exp(log_d - m_t)
    w_inter = jnp.exp(m_inter - m_t)
    s = _dot_nt(q, k) * w_intra
    num = w_inter * _dot(q, c_state) + _dot(s, v)
    den = w_inter * jnp.sum(q * n_row, axis=1, keepdims=True) + jnp.sum(s, axis=1, keepdims=True)
    h = num / jnp.maximum(jnp.abs(den), jnp.exp(-m_t))
    m_new = m_t[valid - 1:valid, :]
    decay = jnp.exp(m_inter[valid - 1:valid, :] - m_new)
    lane = lax.broadcasted_iota(jnp.int32, (1, t), 1)
    wk_row = jnp.where(lane < valid, jnp.exp(b_row[:, valid - 1:valid] - b_row + li_row - m_new), 0.0)
    wk_col = jnp.sum(jnp.where(eye, jnp.broadcast_to(wk_row, (t, t)), 0.0), axis=1, keepdims=True)
    kw = k * wk_col
    c_new = decay * c_state + _dot_tn(kw, v)
    n_new = decay * n_row + jnp.sum(kw, axis=0, keepdims=True)
    return h, c_new, n_new, m_new


def _mlstm_head_step(q, k, v, o, li_row, b_row, norm_row, c_ref, n_ref, m_ref, h_idx, valid):
    dk = q.shape[1]
    c_state = c_ref[h_idx]
    n_row = n_ref[h_idx:h_idx + 1, :]
    m_prev = m_ref[h_idx:h_idx + 1, 0:1]
    h, c_new, n_new, m_new = _mlstm_chunk(q, k * (dk ** -0.5), v, li_row, b_row, c_state, n_row, m_prev, valid)
    c_ref[h_idx] = c_new
    n_ref[h_idx:h_idx + 1, :] = n_new
    m_ref[h_idx:h_idx + 1, :] = jnp.broadcast_to(m_new, (1, m_ref.shape[1]))
    hn = h * lax.rsqrt(jnp.mean(h * h, axis=-1, keepdims=True) + EPS) * norm_row
    return jax.nn.sigmoid(o) * hn


def _mlstm_prompt_kernel(q_ref, k_ref, v_ref, o_ref, li_ref, lf_ref,
                         qm_ref, km_ref, vm_ref, om_ref, lim_ref, lfm_ref, norm_ref,
                         out_ref, outm_ref, c_ref, n_ref, m_ref, *, heads, dk, dv, chunks):
    g = pl.program_id(0)

    @pl.when(g == 0)
    def _():
        c_ref[...] = jnp.zeros_like(c_ref)
        n_ref[...] = jnp.zeros_like(n_ref)
        m_ref[...] = jnp.full_like(m_ref, M_INIT)
        bm = _lane_cumsum(lfm_ref[...])
        for h in range(heads):
            res = _mlstm_head_step(
                qm_ref[:, h * dk:(h + 1) * dk], km_ref[:, h * dk:(h + 1) * dk],
                vm_ref[:, h * dv:(h + 1) * dv], om_ref[:, h * dv:(h + 1) * dv],
                lim_ref[h:h + 1, 0:N_META], bm[h:h + 1, 0:N_META], norm_ref[:, h * dv:(h + 1) * dv],
                c_ref, n_ref, m_ref, h, N_META)
            outm_ref[:, h * dv:(h + 1) * dv] = res.astype(outm_ref.dtype)

    def body(c, carry):
        r0 = pl.multiple_of(c * CHUNK, CHUNK)
        rows = pl.ds(r0, CHUNK)
        gc = g * chunks + c
        for h in range(heads):
            b_row = _lane_cumsum(lf_ref[h, pl.ds(gc, 1), :])[:, 0:CHUNK]
            li_row = li_ref[h, pl.ds(gc, 1), :][:, 0:CHUNK]
            res = _mlstm_head_step(
                q_ref[rows, h * dk:(h + 1) * dk], k_ref[rows, h * dk:(h + 1) * dk],
                v_ref[rows, h * dv:(h + 1) * dv], o_ref[rows, h * dv:(h + 1) * dv],
                li_row, b_row, norm_ref[:, h * dv:(h + 1) * dv],
                c_ref, n_ref, m_ref, h, CHUNK)
            out_ref[rows, h * dv:(h + 1) * dv] = res.astype(out_ref.dtype)
        return carry

    lax.fori_loop(0, chunks, body, 0)


def _mlstm_prompt(z, li, lf, li_meta, lf_meta, norm, seq, heads, dk, dv):
    chunks = 4
    rb = chunks * CHUNK
    qw, vw = heads * dk, heads * dv
    meta_blk = seq // N_META
    spec = lambda w, c: pl.BlockSpec((rb, w), lambda g, c=c: (g, c))
    mspec = lambda w, c: pl.BlockSpec((N_META, w), lambda g, c=c: (meta_blk, c))
    gspec = pl.BlockSpec((heads, seq // CHUNK, LANES), lambda g: (0, 0, 0))
    const2 = lambda shape: pl.BlockSpec(shape, lambda g: (0,) * len(shape))
    return pl.pallas_call(
        functools.partial(_mlstm_prompt_kernel, heads=heads, dk=dk, dv=dv, chunks=chunks),
        grid=(seq // rb,),
        in_specs=[spec(qw, 0), spec(qw, 1), spec(vw, 1), spec(vw, 2), gspec, gspec,
                  mspec(qw, 0), mspec(qw, 1), mspec(vw, 1), mspec(vw, 2),
                  const2((heads, LANES)), const2((heads, LANES)), const2((1, vw))],
        out_specs=[pl.BlockSpec((rb, vw), lambda g: (g, 0)), const2((N_META, vw)),
                   const2((heads, dk, dv)), const2((heads, dk)), const2((heads, LANES))],
        out_shape=[jax.ShapeDtypeStruct((seq, vw), BF16), jax.ShapeDtypeStruct((N_META, vw), BF16),
                   jax.ShapeDtypeStruct((heads, dk, dv), F32), jax.ShapeDtypeStruct((heads, dk), F32),
                   jax.ShapeDtypeStruct((heads, LANES), F32)],
        compiler_params=_params("arbitrary"),
        name="mlstm_prompt",
    )(z, z, z, z, li, lf, z, z, z, z, li_meta, lf_meta, norm)


def _mlstm_sample_kernel(q_ref, k_ref, v_ref, o_ref, li_ref, lf_ref, c0_ref, n0_ref, m0_ref, norm_ref,
                         out_ref, c_ref, n_ref, m_ref, *, heads, dk, dv, valid):
    c_ref[...] = c0_ref[...]
    n_ref[...] = n0_ref[...]
    m_ref[...] = m0_ref[...]
    t = q_ref.shape[0]
    b_all = _lane_cumsum(lf_ref[...])
    for h in range(heads):
        res = _mlstm_head_step(
            q_ref[:, h * dk:(h + 1) * dk], k_ref[:, h * dk:(h + 1) * dk],
            v_ref[:, h * dv:(h + 1) * dv], o_ref[:, h * dv:(h + 1) * dv],
            li_ref[h:h + 1, 0:t], b_all[h:h + 1, 0:t], norm_ref[:, h * dv:(h + 1) * dv],
            c_ref, n_ref, m_ref, h, valid)
        out_ref[:, h * dv:(h + 1) * dv] = res


def _mlstm_sample(zs, li, lf, c0, n0, m0, norm, heads, dk, dv, valid):
    nb, t, _ = zs.shape
    qw, vw = heads * dk, heads * dv
    spec = lambda w, c: pl.BlockSpec((None, t, w), lambda b, c=c: (b, 0, c))
    per_b = lambda *shape: pl.BlockSpec((None,) + shape, lambda b: (b,) + (0,) * len(shape))
    return pl.pallas_call(
        functools.partial(_mlstm_sample_kernel, heads=heads, dk=dk, dv=dv, valid=valid),
        grid=(nb,),
        in_specs=[spec(qw, 0), spec(qw, 1), spec(vw, 1), spec(vw, 2), per_b(heads, LANES), per_b(heads, LANES),
                  per_b(heads, dk, dv), per_b(heads, dk), per_b(heads, LANES),
                  pl.BlockSpec((1, vw), lambda b: (0, 0))],
        out_specs=[per_b(t, vw), per_b(heads, dk, dv), per_b(heads, dk), per_b(heads, LANES)],
        out_shape=[jax.ShapeDtypeStruct((nb, t, vw), F32), jax.ShapeDtypeStruct((nb, heads, dk, dv), F32),
                   jax.ShapeDtypeStruct((nb, heads, dk), F32), jax.ShapeDtypeStruct((nb, heads, LANES), F32)],
        compiler_params=_params("parallel"),
        name="mlstm_sample",
    )(zs, zs, zs, zs, li, lf, c0, n0, m0, norm)


def _lambda_value(lp_ref, lam_init):
    lp = lp_ref[...]
    a = jnp.sum(lp[0:1, :] * lp[1:2, :], axis=1, keepdims=True)
    b = jnp.sum(lp[2:3, :] * lp[3:4, :], axis=1, keepdims=True)
    return jnp.exp(a) - jnp.exp(b) + lam_init


def _softmax_step_t(q, kb, vb, m_ref, l_ref, acc_ref, mask):
    s = _dot_nt(kb, q)
    if mask is not None:
        s = jnp.where(mask, s, MASK_VALUE)
    m_old = m_ref[...]
    m_new = jnp.maximum(m_old, jnp.max(s, axis=0, keepdims=True))
    alpha = jnp.exp(m_old - m_new)
    p = jnp.exp(s - m_new)
    l_ref[...] = alpha * l_ref[...] + jnp.sum(p, axis=0, keepdims=True)
    acc_ref[...] = alpha * acc_ref[...] + _dot_tn(vb, p)
    m_ref[...] = m_new


def _diff_finish_t(lam, sub_ref, l0_ref, a0_ref, l1_ref, a1_ref):
    o_t = a0_ref[...] / l0_ref[...] - lam * (a1_ref[...] / l1_ref[...])
    o = o_t.T
    return o * lax.rsqrt(jnp.mean(o * o, axis=-1, keepdims=True) + EPS) * sub_ref[...]


def _init_softmax_state(m_ref, l_ref, acc_ref):
    m_ref[...] = jnp.full_like(m_ref, MASK_VALUE)
    l_ref[...] = jnp.zeros_like(l_ref)
    acc_ref[...] = jnp.zeros_like(acc_ref)


def _flash_kernel(q0_ref, q1_ref, k_ref, v_ref, lp_ref, sub_ref, out_ref,
                  m0, l0, a0, m1, l1, a1, *, seq, tq, lam_init):
    qi = pl.program_id(1)
    q0 = q0_ref[...]
    q1 = q1_ref[...]
    _init_softmax_state(m0, l0, a0)
    _init_softmax_state(m1, l1, a1)

    def step(kb, vb, mask):
        _softmax_step_t(q0, kb, vb, m0, l0, a0, mask)
        _softmax_step_t(q1, kb, vb, m1, l1, a1, mask)

    step(k_ref[seq:seq + N_META, :], v_ref[seq:seq + N_META, :], None)

    def body(j, carry):
        rows = pl.ds(pl.multiple_of(j * tq, tq), tq)
        step(k_ref[rows, :], v_ref[rows, :], None)
        return carry

    lax.fori_loop(0, qi, body, 0)
    rows = pl.ds(pl.multiple_of(qi * tq, tq), tq)
    ki = lax.broadcasted_iota(jnp.int32, (tq, tq), 0)
    qj = lax.broadcasted_iota(jnp.int32, (tq, tq), 1)
    step(k_ref[rows, :], v_ref[rows, :], ki <= qj)
    out_ref[...] = _diff_finish_t(_lambda_value(lp_ref, lam_init), sub_ref, l0, a0, l1, a1).astype(out_ref.dtype)


def _flash(q0, q1, kb, vb, lam_p, subln, seq, lam_init):
    width = q0.shape[1]
    heads = width // DIFF_DV
    tq = min(512, seq)
    kv_rows = seq + N_META
    qspec = pl.BlockSpec((tq, DIFF_DV), lambda h, i: (i, h))
    kspec = pl.BlockSpec((kv_rows, DIFF_DV), lambda h, i: (0, h))
    stat = pltpu.VMEM((1, tq), F32)
    acc = pltpu.VMEM((DIFF_DV, tq), F32)
    return pl.pallas_call(
        functools.partial(_flash_kernel, seq=seq, tq=tq, lam_init=lam_init),
        grid=(heads, seq // tq),
        in_specs=[qspec, qspec, kspec, kspec,
                  pl.BlockSpec(lam_p.shape, lambda h, i: (0, 0)), pl.BlockSpec((1, DIFF_DV), lambda h, i: (0, 0))],
        out_specs=pl.BlockSpec((tq, DIFF_DV), lambda h, i: (i, h)),
        out_shape=jax.ShapeDtypeStruct((seq, width), BF16),
        scratch_shapes=[stat, stat, acc, stat, stat, acc],
        compiler_params=_params("parallel", "arbitrary"),
        name="diff_attn_prompt",
    )(q0, q1, kb, vb, lam_p, subln)


def _meta_attn_kernel(q0_ref, q1_ref, k_ref, v_ref, lp_ref, sub_ref, out_ref,
                      m0, l0, a0, m1, l1, a1, *, lam_init):
    _init_softmax_state(m0, l0, a0)
    _init_softmax_state(m1, l1, a1)
    t = q0_ref.shape[0]
    ki = lax.broadcasted_iota(jnp.int32, (t, t), 0)
    qj = lax.broadcasted_iota(jnp.int32, (t, t), 1)
    mask = ki <= qj
    _softmax_step_t(q0_ref[...], k_ref[...], v_ref[...], m0, l0, a0, mask)
    _softmax_step_t(q1_ref[...], k_ref[...], v_ref[...], m1, l1, a1, mask)
    out_ref[...] = _diff_finish_t(_lambda_value(lp_ref, lam_init), sub_ref, l0, a0, l1, a1)


def _meta_attn(q0, q1, kb, vb, lam_p, subln, lam_init):
    t, width = q0.shape
    heads = width // DIFF_DV
    spec = pl.BlockSpec((t, DIFF_DV), lambda h: (0, h))
    stat = pltpu.VMEM((1, t), F32)
    acc = pltpu.VMEM((DIFF_DV, t), F32)
    return pl.pallas_call(
        functools.partial(_meta_attn_kernel, lam_init=lam_init),
        grid=(heads,),
        in_specs=[spec, spec, spec, spec,
                  pl.BlockSpec(lam_p.shape, lambda h: (0, 0)), pl.BlockSpec((1, DIFF_DV), lambda h: (0, 0))],
        out_specs=spec,
        out_shape=jax.ShapeDtypeStruct((t, width), F32),
        scratch_shapes=[stat, stat, acc, stat, stat, acc],
        compiler_params=_params("parallel"),
        name="diff_attn_meta",
    )(q0, q1, kb, vb, lam_p, subln)


def _sample_attn_kernel(pt_ref, q_ref, *refs, pages, heads, new_valid, q_len, lam_init):
    k_refs = refs[:pages]
    v_refs = refs[pages:2 * pages]
    kn_ref, vn_ref, lp_ref, sub_ref, out_ref, kcat, vcat, m_s, l_s, acc_s = refs[2 * pages:]
    j = pl.program_id(1)
    page = k_refs[0].shape[0]
    rows_q = q_ref.shape[0]
    group = rows_q // heads

    @pl.when(j == 0)
    def _():
        m_s[...] = jnp.full_like(m_s, MASK_VALUE)
        l_s[...] = jnp.zeros_like(l_s)
        acc_s[...] = jnp.zeros_like(acc_s)

    q = q_ref[...]

    def update(kb, vb, mask):
        s = _dot_nt(q, kb)
        if mask is not None:
            s = jnp.where(mask, s, MASK_VALUE)
        m_old = m_s[...]
        m_new = jnp.maximum(m_old, jnp.max(s, axis=1, keepdims=True))
        alpha = jnp.exp(m_old - m_new)
        p = jnp.exp(s - m_new)
        l_s[...] = alpha * l_s[...] + jnp.sum(p, axis=1, keepdims=True)
        pv = _dot(p, vb)
        diag = jnp.concatenate(
            [pv[h * group:(h + 1) * group, h * DIFF_DV:(h + 1) * DIFF_DV] for h in range(heads)], axis=0)
        acc_s[...] = alpha * acc_s[...] + diag
        m_s[...] = m_new

    for i in range(pages):
        kcat[i * page:(i + 1) * page, :] = k_refs[i][...].astype(BF16)
        vcat[i * page:(i + 1) * page, :] = v_refs[i][...].astype(BF16)
    update(kcat[...], vcat[...], None)

    @pl.when(j == pl.num_programs(1) - 1)
    def _():
        n_new = kn_ref.shape[0]
        r = lax.broadcasted_iota(jnp.int32, (rows_q, n_new), 0)
        u = lax.broadcasted_iota(jnp.int32, (rows_q, n_new), 1)
        update(kn_ref[...], vn_ref[...], (u <= r % q_len) & (u < new_valid))
        on = acc_s[...] / l_s[...]
        lam = _lambda_value(lp_ref, lam_init)
        d = on - lam * pltpu.roll(on, rows_q - q_len, 0)
        out_ref[...] = d * lax.rsqrt(jnp.mean(d * d, axis=-1, keepdims=True) + EPS) * sub_ref[...]


def _sample_attn(page_table, qbd, cache_k, cache_v, k_new, v_new, lam_p, subln, q_len, lam_init):
    nb, rows_q, width = qbd.shape
    heads = width // DIFF_DV
    page = cache_k.shape[1]
    n_pages = page_table.shape[1]
    pages = PAGES_PER_STEP
    page_spec = lambda i: pl.BlockSpec((None, page, width), lambda b, j, pt, i=i: (pt[b, j * pages + i], 0, 0))
    per_b = lambda r: pl.BlockSpec((None, r, width), lambda b, j, pt: (b, 0, 0))
    grid_spec = pltpu.PrefetchScalarGridSpec(
        num_scalar_prefetch=1,
        grid=(nb, n_pages // pages),
        in_specs=[per_b(rows_q)] + [page_spec(i) for i in range(pages)] * 2
                 + [per_b(k_new.shape[1]), per_b(v_new.shape[1]),
                    pl.BlockSpec(lam_p.shape, lambda b, j, pt: (0, 0)),
                    pl.BlockSpec((1, DIFF_DV), lambda b, j, pt: (0, 0))],
        out_specs=pl.BlockSpec((None, rows_q, DIFF_DV), lambda b, j, pt: (b, 0, 0)),
        scratch_shapes=[pltpu.VMEM((pages * page, width), BF16), pltpu.VMEM((pages * page, width), BF16),
                        pltpu.VMEM((rows_q, 1), F32), pltpu.VMEM((rows_q, 1), F32),
                        pltpu.VMEM((rows_q, DIFF_DV), F32)],
    )
    return pl.pallas_call(
        functools.partial(_sample_attn_kernel, pages=pages, heads=heads, new_valid=q_len, q_len=q_len,
                          lam_init=lam_init),
        grid_spec=grid_spec,
        out_shape=jax.ShapeDtypeStruct((nb, rows_q, DIFF_DV), F32),
        compiler_params=_params("parallel", "arbitrary"),
        name="diff_attn_sample",
    )(page_table, qbd, *([cache_k] * pages), *([cache_v] * pages), k_new, v_new, lam_p, subln)


def _rope_tables(pos):
    half = DIFF_DH // 2
    inv = ROPE_THETA ** (-jnp.arange(half, dtype=F32) / half)
    ang = jnp.asarray(pos, F32)[:, None] * inv[None, :]
    cos, sin = jnp.cos(ang), jnp.sin(ang)
    cos = jnp.tile(cos, (1, LANES // half))
    sin = jnp.tile(jnp.concatenate([-sin, sin], axis=1), (1, LANES // DIFF_DH))
    return cos, sin


def _pad_cols(w, n):
    return jnp.pad(w, ((0, 0), (0, n - w.shape[1])))


def kernel(x_prompt, x_sample, cache_k, cache_v, state_C, state_n, state_m, page_table, meta_tokens,
           norm_ffn1, ffn1_w_gate, ffn1_w_up, ffn1_w_down, norm_mix, w_in, mlstm_b_i, mlstm_b_f,
           mlstm_norm, diff_lambda_qk, diff_subln, w_out, norm_ffn2, ffn2_w_gate, ffn2_w_up,
           ffn2_w_down, norm_final):
    batch, seq, d_model = x_prompt.shape
    assert batch == 1 and state_C.shape[0] == 1
    dec_b, dec_t, _ = x_sample.shape
    heads_m = mlstm_b_i.shape[1]
    dk, dv = state_C.shape[3], state_C.shape[4]
    qw, vw = heads_m * dk, heads_m * dv
    dw = w_out.shape[1] - vw
    heads_d = dw // DIFF_DV
    assert vw == 2 * qw and dw == vw, "column blocks of the projection are addressed in units of the q width"
    n_s = dec_b * dec_t
    rows = seq + N_META + n_s
    past = page_table.shape[1] * cache_k.shape[2]
    lam_init = 0.8 - 0.6 * math.exp(-0.0)

    def ffn_weights(wg, wu, wd):
        d_ff = wg.shape[1]
        d_pad = _round_up(d_ff, 512)
        return (_pad_cols(wg.astype(BF16), d_pad), _pad_cols(wu.astype(BF16), d_pad),
                jnp.pad(wd.astype(BF16), ((0, d_pad - d_ff), (0, 0))))

    h = jnp.concatenate([x_prompt[0], meta_tokens.astype(F32), x_sample.reshape(n_s, d_model)], axis=0)

    h = _ffn(h, norm_ffn1[0], *ffn_weights(ffn1_w_gate[0], ffn1_w_up[0], ffn1_w_down[0]))

    wi = w_in[0]
    c_gate = 2 * qw + 2 * vw
    w_main = jnp.concatenate([wi[:, :c_gate], wi[:, c_gate + 2 * heads_m:]], axis=1).astype(BF16)
    w_gate = _pad_cols(wi[:, c_gate:c_gate + 2 * heads_m], LANES).astype(BF16)
    u = _rmsnorm(h, norm_mix[0], BF16, 256)
    tm = _row_tile(rows, 1088)
    z = _mm(u, w_main, tm, 512)
    zg = _mm(u, w_gate, tm, LANES)
    gate_bias = jnp.pad(jnp.concatenate([mlstm_b_i[0], mlstm_b_f[0]]), (0, LANES - 2 * heads_m)).reshape(1, LANES)
    gates = _gates(zg, gate_bias.astype(F32), heads_m)
    log_i, log_f = gates[:, :heads_m], gates[:, heads_m:2 * heads_m]

    def chunk_rows(a):
        a = a.T.reshape(heads_m, seq // CHUNK, CHUNK)
        return jnp.pad(a, ((0, 0), (0, 0), (0, LANES - CHUNK)))

    lane_rows = lambda a: jnp.pad(a, ((0, 0),) * (a.ndim - 1) + ((0, LANES - a.shape[-1]),))
    norm_m = mlstm_norm[0].reshape(1, vw).astype(F32)
    mo_p, mo_meta, c_p, n_p, m_p = _mlstm_prompt(
        z, chunk_rows(log_i[:seq]), chunk_rows(log_f[:seq]),
        lane_rows(log_i[seq:seq + N_META].T), lane_rows(log_f[seq:seq + N_META].T), norm_m, seq, heads_m, dk, dv)

    t_pad = 8
    zs = jnp.pad(z[seq + N_META:, :c_gate].reshape(dec_b, dec_t, c_gate), ((0, 0), (0, t_pad - dec_t), (0, 0)))
    gate_s = lambda a: lane_rows(a[seq + N_META:].reshape(dec_b, dec_t, heads_m).transpose(0, 2, 1))
    m0 = jnp.broadcast_to(state_m[0][:, :, None], (dec_b, heads_m, LANES))
    mo_s, c_s, n_s_new, m_s = _mlstm_sample(zs, gate_s(log_i), gate_s(log_f), state_C[0], state_n[0], m0, norm_m,
                                             heads_m, dk, dv, dec_t)

    pos = np.concatenate([N_META + np.arange(seq), np.arange(N_META), past + np.tile(np.arange(dec_t), dec_b)])
    cos, sin = _rope_tables(pos)
    blk = c_gate // dw
    q0, q1, k_rot, kb, vb = _rope(z, blk, blk + 1, blk + 2, dw, cos, sin)
    sub = (diff_subln[0].astype(F32) * (1.0 - lam_init)).reshape(1, DIFF_DV)
    lam_p = diff_lambda_qk[0].astype(F32)
    do_p = _flash(q0, q1, kb, vb, lam_p, sub, seq, lam_init)

    pad128 = lambda a: jnp.pad(a, ((0, LANES - a.shape[0]), (0, 0)))
    meta = slice(seq, seq + N_META)
    do_meta = _meta_attn(pad128(q0[meta]), pad128(q1[meta]), pad128(kb[meta]), pad128(vb[meta]), lam_p, sub,
                         lam_init)[:N_META]

    qs = (q0[seq + N_META:] + q1[seq + N_META:]).reshape(dec_b, 1, dec_t, dw)
    rows_q = heads_d * 2 * dec_t
    qbd = jnp.broadcast_to(qs, (dec_b, heads_d * 2, dec_t, dw)).reshape(dec_b, rows_q, dw)
    r_id = np.arange(rows_q)[:, None] // dec_t
    c_id = np.arange(dw)[None, :] // DIFF_DH
    qbd = jnp.where(jnp.asarray(r_id == c_id), qbd, jnp.zeros_like(qbd))
    new_rows = lambda a: jnp.pad(a[seq + N_META:].reshape(dec_b, dec_t, dw), ((0, 0), (0, LANES - dec_t), (0, 0)))
    n_phys, page = cache_k.shape[1], cache_k.shape[2]
    do_s = _sample_attn(page_table, qbd, cache_k[0].reshape(n_phys, page, dw), cache_v[0].reshape(n_phys, page, dw),
                        new_rows(kb), new_rows(vb), lam_p, sub, dec_t, lam_init)
    do_s = do_s.reshape(dec_b, heads_d, 2, dec_t, DIFF_DV)[:, :, 0].transpose(0, 2, 1, 3).reshape(n_s, dw)

    mix = jnp.concatenate([
        jnp.concatenate([mo_p, mo_meta, mo_s[:, :dec_t].reshape(n_s, vw).astype(BF16)], axis=0),
        jnp.concatenate([do_p, do_meta.astype(BF16), do_s.astype(BF16)], axis=0)], axis=1)
    h = _mm_resid(mix, w_out[0].astype(BF16), h, 1.0, _row_tile(rows, 1088), 512)

    h = _ffn(h, norm_ffn2[0], *ffn_weights(ffn2_w_gate[0], ffn2_w_up[0], ffn2_w_down[0]))

    y_prompt = _rmsnorm(h, norm_final, F32, 256, 0, seq // 256).reshape(1, seq, d_model)
    y_sample = _rmsnorm(h, norm_final, F32, N_META, (seq + N_META) // N_META, n_s // N_META)
    y_sample = y_sample.reshape(dec_b, dec_t, d_model)

    order = lambda a: jnp.concatenate([a[meta], a[:seq]], axis=0)
    v_all = z[:, (blk + 2) * dw:(blk + 3) * dw]
    k_prompt = order(k_rot).reshape(1, 1, seq + N_META, heads_d, 2, DIFF_DH)
    v_prompt = order(v_all).reshape(1, 1, seq + N_META, heads_d, DIFF_DV)
    k_sample = k_rot[seq + N_META:].reshape(1, dec_b, dec_t, heads_d, 2, DIFF_DH)
    v_sample = v_all[seq + N_META:].reshape(1, dec_b, dec_t, heads_d, DIFF_DV)
    return (y_prompt, y_sample, k_prompt, v_prompt,
            c_p[None, None], n_p[None, None], m_p[None, None, :, 0],
            k_sample, v_sample, c_s[None], n_s_new[None], m_s[None, :, :, 0])
```

```python
import functools
import math

import numpy as np
import jax
import jax.numpy as jnp
from jax import lax
from jax.experimental import pallas as pl
from jax.experimental.pallas import tpu as pltpu

F32 = jnp.float32
BF16 = jnp.bfloat16

N_META = 16
CHUNK = 64
ROPE_THETA = 10000.0
GATE_CAP = 15.0
M_INIT = -1e30
EPS = 1e-6
DIFF_DV = 128
DIFF_DH = DIFF_DV // 2
DIFF_SCALE = DIFF_DH ** -0.5
Q_SCALE_LOG2 = DIFF_SCALE * math.log2(math.e)
PAGES_PER_STEP = 4
FLASH_Q_TILE = 1024
FLASH_K_TILE = 512
MASK_VALUE = -1e30

LANES = 128
V7X_VMEM_BYTES = 64 * 2 ** 20
VMEM_LIMIT = V7X_VMEM_BYTES - 8 * 2 ** 20


def _params(*sem):
    return pltpu.CompilerParams(dimension_semantics=sem, vmem_limit_bytes=VMEM_LIMIT)


def _round_up(x, m):
    return (x + m - 1) // m * m


def _row_tile(rows, target):
    n = pl.cdiv(rows, target)
    return _round_up(pl.cdiv(rows, n), 16)


def _dot(a, b):
    return jnp.dot(a.astype(BF16), b.astype(BF16), preferred_element_type=F32)


def _dot_nt(a, b):
    return lax.dot_general(a.astype(BF16), b.astype(BF16), (((1,), (1,)), ((), ())),
                           preferred_element_type=F32)


def _dot_tn(a, b):
    return lax.dot_general(a.astype(BF16), b.astype(BF16), (((0,), (0,)), ((), ())),
                           preferred_element_type=F32)


def _rmsnorm_kernel(x_ref, g_ref, o_ref):
    x = x_ref[...]
    y = x * lax.rsqrt(jnp.mean(x * x, axis=-1, keepdims=True) + EPS)
    o_ref[...] = (y * g_ref[...]).astype(o_ref.dtype)


def _rmsnorm(x, g, out_dtype, row_block, first_block=0, n_blocks=None):
    rows, d = x.shape
    if n_blocks is None:
        n_blocks, out_rows = pl.cdiv(rows, row_block), rows
    else:
        out_rows = n_blocks * row_block
    return pl.pallas_call(
        _rmsnorm_kernel,
        grid=(n_blocks,),
        in_specs=[pl.BlockSpec((row_block, d), lambda i: (i + first_block, 0)),
                  pl.BlockSpec((1, d), lambda i: (0, 0))],
        out_specs=pl.BlockSpec((row_block, d), lambda i: (i, 0)),
        out_shape=jax.ShapeDtypeStruct((out_rows, d), out_dtype),
        compiler_params=_params("parallel"),
        name="rmsnorm",
    )(x, g.reshape(1, d).astype(F32))


def _gateup_kernel(x_ref, wg_ref, wu_ref, o_ref):
    x = x_ref[...]
    g = jnp.dot(x, wg_ref[...], preferred_element_type=F32)
    u = jnp.dot(x, wu_ref[...], preferred_element_type=F32)
    o_ref[...] = (g * jax.nn.sigmoid(g) * u).astype(o_ref.dtype)


def _gateup(x, wg, wu, tm, tn):
    rows, d = x.shape
    n = wg.shape[1]
    return pl.pallas_call(
        _gateup_kernel,
        grid=(pl.cdiv(rows, tm), n // tn),
        in_specs=[pl.BlockSpec((tm, d), lambda i, j: (i, 0)),
                  pl.BlockSpec((d, tn), lambda i, j: (0, j)),
                  pl.BlockSpec((d, tn), lambda i, j: (0, j))],
        out_specs=pl.BlockSpec((tm, tn), lambda i, j: (i, j)),
        out_shape=jax.ShapeDtypeStruct((rows, n), BF16),
        compiler_params=_params("parallel", "arbitrary"),
        name="ffn_gate_up",
    )(x, wg, wu)


def _mm_resid_kernel(a_ref, w_ref, r_ref, o_ref, *, scale):
    acc = jnp.dot(a_ref[...], w_ref[...], preferred_element_type=F32)
    o_ref[...] = r_ref[...] + scale * acc


def _mm_resid(a, w, resid, scale, tm, tn):
    rows, k = a.shape
    n = w.shape[1]
    return pl.pallas_call(
        functools.partial(_mm_resid_kernel, scale=scale),
        grid=(pl.cdiv(rows, tm), n // tn),
        in_specs=[pl.BlockSpec((tm, k), lambda i, j: (i, 0)),
                  pl.BlockSpec((k, tn), lambda i, j: (0, j)),
                  pl.BlockSpec((tm, tn), lambda i, j: (i, j))],
        out_specs=pl.BlockSpec((tm, tn), lambda i, j: (i, j)),
        out_shape=jax.ShapeDtypeStruct((rows, n), F32),
        compiler_params=_params("parallel", "arbitrary"),
        name="matmul_residual",
    )(a, w, resid)


def _mm_kernel(a_ref, w_ref, o_ref):
    o_ref[...] = jnp.dot(a_ref[...], w_ref[...], preferred_element_type=F32)


def _mm(a, w, tm, tn):
    rows, k = a.shape
    n = w.shape[1]
    return pl.pallas_call(
        _mm_kernel,
        grid=(pl.cdiv(rows, tm), n // tn),
        in_specs=[pl.BlockSpec((tm, k), lambda i, j: (i, 0)),
                  pl.BlockSpec((k, tn), lambda i, j: (0, j))],
        out_specs=pl.BlockSpec((tm, tn), lambda i, j: (i, j)),
        out_shape=jax.ShapeDtypeStruct((rows, n), F32),
        compiler_params=_params("parallel", "arbitrary"),
        name="matmul",
    )(a, w)


def _ffn(h, g, wg, wu, wd):
    rows = h.shape[0]
    xn = _rmsnorm(h, g, BF16, 256)
    hid = _gateup(xn, wg, wu, _row_tile(rows, 1088), 512 if wg.shape[1] % 512 == 0 else 256)
    return _mm_resid(hid, wd, h, 0.5, _row_tile(rows, 544), 256)


def _gate_kernel(z_ref, b_ref, o_ref, *, heads):
    z = z_ref[...] + b_ref[...]
    c = GATE_CAP * jnp.tanh(z / GATE_CAP)
    log_sig = jnp.minimum(c, 0.0) - jnp.log1p(jnp.exp(-jnp.abs(c)))
    lane = lax.broadcasted_iota(jnp.int32, c.shape, 1)
    o_ref[...] = jnp.where(lane < heads, c, log_sig)


def _gates(z, bias, heads):
    rows, d = z.shape
    rb = 512
    return pl.pallas_call(
        functools.partial(_gate_kernel, heads=heads),
        grid=(pl.cdiv(rows, rb),),
        in_specs=[pl.BlockSpec((rb, d), lambda i: (i, 0)), pl.BlockSpec((1, d), lambda i: (0, 0))],
        out_specs=pl.BlockSpec((rb, d), lambda i: (i, 0)),
        out_shape=jax.ShapeDtypeStruct((rows, d), F32),
        compiler_params=_params("parallel"),
        name="mlstm_gates",
    )(z, bias)


def _rope_kernel(q_ref, k_ref, v_ref, cos_ref, sin_ref, q0_ref, q1_ref, kf_ref, kb_ref, vb_ref):
    cos = cos_ref[...]
    sin = sin_ref[...]
    lane = lax.broadcasted_iota(jnp.int32, cos.shape, 1)
    first_half = (lane % DIFF_DH) < (DIFF_DH // 2)
    map0 = lane < DIFF_DH

    def rot(x):
        partner = jnp.where(first_half, pltpu.roll(x, LANES - DIFF_DH // 2, 1), pltpu.roll(x, DIFF_DH // 2, 1))
        return x * cos + partner * sin

    for j in range(q_ref.shape[1] // LANES):
        sl = slice(j * LANES, (j + 1) * LANES)
        q = rot(q_ref[:, sl]) * Q_SCALE_LOG2
        q0_ref[:, sl] = jnp.where(map0, q, 0.0).astype(BF16)
        q1_ref[:, sl] = jnp.where(map0, 0.0, q).astype(BF16)
        k = rot(k_ref[:, sl])
        kf_ref[:, sl] = k
        kb_ref[:, sl] = k.astype(BF16)
    vb_ref[...] = v_ref[...].astype(BF16)


def _rope(z, col_q, col_k, col_v, width, cos, sin):
    rows = z.shape[0]
    rb = 256
    zspec = lambda c: pl.BlockSpec((rb, width), lambda i, c=c: (i, c))
    tspec = pl.BlockSpec((rb, LANES), lambda i: (i, 0))
    ospec = pl.BlockSpec((rb, width), lambda i: (i, 0))
    bshape = jax.ShapeDtypeStruct((rows, width), BF16)
    return pl.pallas_call(
        _rope_kernel,
        grid=(pl.cdiv(rows, rb),),
        in_specs=[zspec(col_q), zspec(col_k), zspec(col_v), tspec, tspec],
        out_specs=[ospec] * 5,
        out_shape=[bshape, bshape, jax.ShapeDtypeStruct((rows, width), F32), bshape, bshape],
        compiler_params=_params("parallel"),
        name="rope",
    )(z, z, z, cos, sin)


def _lane_cumsum(x):
    lane = lax.broadcasted_iota(jnp.int32, x.shape, 1)
    k = 1
    while k < x.shape[1]:
        x = x + jnp.where(lane >= k, pltpu.roll(x, k, 1), 0.0)
        k *= 2
    return x


def _mlstm_heads(load, store, c_ref, n_ref, m_ref, heads, valid):
    ins = [load(h) for h in range(heads)]
    t, dk = ins[0][0].shape
    ri = lax.broadcasted_iota(jnp.int32, (t, t), 0)
    ci = lax.broadcasted_iota(jnp.int32, (t, t), 1)
    eye = ri == ci
    lane = lax.broadcasted_iota(jnp.int32, (1, t), 1)

    stage1 = []
    for h, (q, k, v, o, li_row, b_row, norm_row) in enumerate(ins):
        k = k * (dk ** -0.5)
        c_state = c_ref[h]
        stage1.append((k, c_state, _dot_nt(q, k), _dot(q, c_state)))

    stage2 = []
    for h, (q, k_raw, v, o, li_row, b_row, norm_row) in enumerate(ins):
        k, c_state, qk, qc = stage1[h]
        m_prev = m_ref[h:h + 1, 0:1]
        b_mat = jnp.broadcast_to(b_row, (t, t))
        b_col = jnp.sum(jnp.where(eye, b_mat, 0.0), axis=1, keepdims=True)
        log_d = jnp.where(ci <= ri, b_col - b_mat + li_row, -jnp.inf)
        m_inter = m_prev + b_col
        m_t = jnp.maximum(m_inter, jnp.max(log_d, axis=1, keepdims=True))
        w_inter = jnp.exp(m_inter - m_t)
        s = qk * jnp.exp(log_d - m_t)
        m_new = m_t[valid - 1:valid, :]
        decay = jnp.exp(m_inter[valid - 1:valid, :] - m_new)
        wk_row = jnp.where(lane < valid, jnp.exp(b_row[:, valid - 1:valid] - b_row + li_row - m_new), 0.0)
        wk_col = jnp.sum(jnp.where(eye, jnp.broadcast_to(wk_row, (t, t)), 0.0), axis=1, keepdims=True)
        stage2.append((s, k * wk_col, w_inter, m_t, m_new, decay))

    stage3 = []
    for h, (q, k_raw, v, o, li_row, b_row, norm_row) in enumerate(ins):
        s, kw = stage2[h][:2]
        stage3.append((_dot(s, v), _dot_tn(kw, v)))

    for h, (q, k_raw, v, o, li_row, b_row, norm_row) in enumerate(ins):
        k, c_state, qk, qc = stage1[h]
        s, kw, w_inter, m_t, m_new, decay = stage2[h]
        sv, kv = stage3[h]
        n_row = n_ref[h:h + 1, :]
        num = w_inter * qc + sv
        den = w_inter * jnp.sum(q * n_row, axis=1, keepdims=True) + jnp.sum(s, axis=1, keepdims=True)
        y = num / jnp.maximum(jnp.abs(den), jnp.exp(-m_t))
        c_ref[h] = decay * c_state + kv
        n_ref[h:h + 1, :] = decay * n_row + jnp.sum(kw, axis=0, keepdims=True)
        m_ref[h:h + 1, :] = jnp.broadcast_to(m_new, (1, m_ref.shape[1]))
        yn = y * lax.rsqrt(jnp.mean(y * y, axis=-1, keepdims=True) + EPS) * norm_row
        store(h, jax.nn.sigmoid(o) * yn)


def _mlstm_prompt_kernel(q_ref, k_ref, v_ref, o_ref, li_ref, lf_ref,
                         qm_ref, km_ref, vm_ref, om_ref, lim_ref, lfm_ref, norm_ref,
                         out_ref, outm_ref, c_ref, n_ref, m_ref, *, heads, dk, dv, chunks):
    g = pl.program_id(0)

    @pl.when(g == 0)
    def _():
        c_ref[...] = jnp.zeros_like(c_ref)
        n_ref[...] = jnp.zeros_like(n_ref)
        m_ref[...] = jnp.full_like(m_ref, M_INIT)
        bm = _lane_cumsum(lfm_ref[...])

        def load_meta(h):
            return (qm_ref[:, h * dk:(h + 1) * dk], km_ref[:, h * dk:(h + 1) * dk],
                    vm_ref[:, h * dv:(h + 1) * dv], om_ref[:, h * dv:(h + 1) * dv],
                    lim_ref[h:h + 1, 0:N_META], bm[h:h + 1, 0:N_META], norm_ref[:, h * dv:(h + 1) * dv])

        def store_meta(h, y):
            outm_ref[:, h * dv:(h + 1) * dv] = y.astype(outm_ref.dtype)

        _mlstm_heads(load_meta, store_meta, c_ref, n_ref, m_ref, heads, N_META)

    def body(c, carry):
        r0 = pl.multiple_of(c * CHUNK, CHUNK)
        rows = pl.ds(r0, CHUNK)
        gc = g * chunks + c

        def load(h):
            return (q_ref[rows, h * dk:(h + 1) * dk], k_ref[rows, h * dk:(h + 1) * dk],
                    v_ref[rows, h * dv:(h + 1) * dv], o_ref[rows, h * dv:(h + 1) * dv],
                    li_ref[h, pl.ds(gc, 1), :][:, 0:CHUNK], _lane_cumsum(lf_ref[h, pl.ds(gc, 1), :])[:, 0:CHUNK],
                    norm_ref[:, h * dv:(h + 1) * dv])

        def store(h, y):
            out_ref[rows, h * dv:(h + 1) * dv] = y.astype(out_ref.dtype)

        _mlstm_heads(load, store, c_ref, n_ref, m_ref, heads, CHUNK)
        return carry

    lax.fori_loop(0, chunks, body, 0)


def _mlstm_prompt(z, li, lf, li_meta, lf_meta, norm, seq, heads, dk, dv):
    chunks = 4
    rb = chunks * CHUNK
    qw, vw = heads * dk, heads * dv
    meta_blk = seq // N_META
    spec = lambda w, c: pl.BlockSpec((rb, w), lambda g, c=c: (g, c))
    mspec = lambda w, c: pl.BlockSpec((N_META, w), lambda g, c=c: (meta_blk, c))
    gspec = pl.BlockSpec((heads, seq // CHUNK, LANES), lambda g: (0, 0, 0))
    const2 = lambda shape: pl.BlockSpec(shape, lambda g: (0,) * len(shape))
    return pl.pallas_call(
        functools.partial(_mlstm_prompt_kernel, heads=heads, dk=dk, dv=dv, chunks=chunks),
        grid=(seq // rb,),
        in_specs=[spec(qw, 0), spec(qw, 1), spec(vw, 1), spec(vw, 2), gspec, gspec,
                  mspec(qw, 0), mspec(qw, 1), mspec(vw, 1), mspec(vw, 2),
                  const2((heads, LANES)), const2((heads, LANES)), const2((1, vw))],
        out_specs=[pl.BlockSpec((rb, vw), lambda g: (g, 0)), const2((N_META, vw)),
                   const2((heads, dk, dv)), const2((heads, dk)), const2((heads, LANES))],
        out_shape=[jax.ShapeDtypeStruct((seq, vw), BF16), jax.ShapeDtypeStruct((N_META, vw), BF16),
                   jax.ShapeDtypeStruct((heads, dk, dv), F32), jax.ShapeDtypeStruct((heads, dk), F32),
                   jax.ShapeDtypeStruct((heads, LANES), F32)],
        compiler_params=_params("arbitrary"),
        name="mlstm_prompt",
    )(z, z, z, z, li, lf, z, z, z, z, li_meta, lf_meta, norm)


def _mlstm_sample_kernel(q_ref, k_ref, v_ref, o_ref, li_ref, lf_ref, c0_ref, n0_ref, m0_ref, norm_ref,
                         out_ref, c_ref, n_ref, m_ref, *, heads, dk, dv, valid):
    c_ref[...] = c0_ref[...]
    n_ref[...] = n0_ref[...]
    m_ref[...] = m0_ref[...]
    t = q_ref.shape[0]
    b_all = _lane_cumsum(lf_ref[...])

    def load(h):
        return (q_ref[:, h * dk:(h + 1) * dk], k_ref[:, h * dk:(h + 1) * dk],
                v_ref[:, h * dv:(h + 1) * dv], o_ref[:, h * dv:(h + 1) * dv],
                li_ref[h:h + 1, 0:t], b_all[h:h + 1, 0:t], norm_ref[:, h * dv:(h + 1) * dv])

    def store(h, y):
        out_ref[:, h * dv:(h + 1) * dv] = y

    _mlstm_heads(load, store, c_ref, n_ref, m_ref, heads, valid)


def _mlstm_sample(zs, li, lf, c0, n0, m0, norm, heads, dk, dv, valid):
    nb, t, _ = zs.shape
    qw, vw = heads * dk, heads * dv
    spec = lambda w, c: pl.BlockSpec((None, t, w), lambda b, c=c: (b, 0, c))
    per_b = lambda *shape: pl.BlockSpec((None,) + shape, lambda b: (b,) + (0,) * len(shape))
    return pl.pallas_call(
        functools.partial(_mlstm_sample_kernel, heads=heads, dk=dk, dv=dv, valid=valid),
        grid=(nb,),
        in_specs=[spec(qw, 0), spec(qw, 1), spec(vw, 1), spec(vw, 2), per_b(heads, LANES), per_b(heads, LANES),
                  per_b(heads, dk, dv), per_b(heads, dk), per_b(heads, LANES),
                  pl.BlockSpec((1, vw), lambda b: (0, 0))],
        out_specs=[per_b(t, vw), per_b(heads, dk, dv), per_b(heads, dk), per_b(heads, LANES)],
        out_shape=[jax.ShapeDtypeStruct((nb, t, vw), F32), jax.ShapeDtypeStruct((nb, heads, dk, dv), F32),
                   jax.ShapeDtypeStruct((nb, heads, dk), F32), jax.ShapeDtypeStruct((nb, heads, LANES), F32)],
        compiler_params=_params("parallel"),
        name="mlstm_sample",
    )(zs, zs, zs, zs, li, lf, c0, n0, m0, norm)


def _lambda_value(lp_ref, lam_init):
    lp = lp_ref[...]
    a = jnp.sum(lp[0:1, :] * lp[1:2, :], axis=1, keepdims=True)
    b = jnp.sum(lp[2:3, :] * lp[3:4, :], axis=1, keepdims=True)
    return jnp.exp(a) - jnp.exp(b) + lam_init


def _scores_t(q, kb, mask):
    s = _dot_nt(kb, q)
    return s if mask is None else jnp.where(mask, s, MASK_VALUE)


def _softmax_step_t(q, kb, vb, m_ref, l_ref, acc_ref, mask):
    _softmax_update_t(_scores_t(q, kb, mask), vb, m_ref, l_ref, acc_ref)


def _softmax_update_t(s, vb, m_ref, l_ref, acc_ref, cols=slice(None)):
    m_old = m_ref[:, cols]
    m_new = jnp.maximum(m_old, jnp.max(s, axis=0, keepdims=True))
    alpha = jnp.exp2(m_old - m_new)
    p = jnp.exp2(s - m_new)
    l_ref[:, cols] = alpha * l_ref[:, cols] + jnp.sum(p, axis=0, keepdims=True)
    acc_ref[:, cols] = alpha * acc_ref[:, cols] + _dot_tn(vb, p)
    m_ref[:, cols] = m_new


def _diff_finish_t(lam, sub_ref, l0_ref, a0_ref, l1_ref, a1_ref):
    o_t = a0_ref[...] / l0_ref[...] - lam * (a1_ref[...] / l1_ref[...])
    o = o_t.T
    return o * lax.rsqrt(jnp.mean(o * o, axis=-1, keepdims=True) + EPS) * sub_ref[...]


def _init_softmax_state(m_ref, l_ref, acc_ref):
    m_ref[...] = jnp.full_like(m_ref, MASK_VALUE)
    l_ref[...] = jnp.zeros_like(l_ref)
    acc_ref[...] = jnp.zeros_like(acc_ref)


def _flash_kernel(q0_ref, q1_ref, k_ref, v_ref, lp_ref, sub_ref, out_ref, *state, seq, tq, tk, lam_init):
    qi = pl.program_id(1)
    groups = tq // tk
    chains = [state[6 * g:6 * g + 6] for g in range(groups)]
    km, vm = k_ref[seq:seq + N_META, :], v_ref[seq:seq + N_META, :]
    queries = []
    for g, (m0, l0, a0, m1, l1, a1) in enumerate(chains):
        q0 = q0_ref[g * tk:(g + 1) * tk, :]
        q1 = q1_ref[g * tk:(g + 1) * tk, :]
        queries.append((q0, q1))
        _init_softmax_state(m0, l0, a0)
        _init_softmax_state(m1, l1, a1)
        _softmax_step_t(q0, km, vm, m0, l0, a0, None)
        _softmax_step_t(q1, km, vm, m1, l1, a1, None)

    ki = lax.broadcasted_iota(jnp.int32, (tk, tk), 0)
    qj = lax.broadcasted_iota(jnp.int32, (tk, tk), 1)
    causal = ki <= qj

    def step(j, first_group, diagonal):
        rows = pl.ds(pl.multiple_of(j * tk, tk), tk)
        kb, vb = k_ref[rows, :], v_ref[rows, :]
        scores = []
        for g in range(first_group, groups):
            mask = causal if (diagonal and g == first_group) else None
            scores.append((_scores_t(queries[g][0], kb, mask), _scores_t(queries[g][1], kb, mask)))
        for g, (s0, s1) in zip(range(first_group, groups), scores):
            m0, l0, a0, m1, l1, a1 = chains[g]
            _softmax_update_t(s0, vb, m0, l0, a0)
            _softmax_update_t(s1, vb, m1, l1, a1)

    def body(j, carry):
        step(j, 0, False)
        return carry

    lax.fori_loop(0, qi * groups, body, 0)
    for g in range(groups):
        step(qi * groups + g, g, True)

    lam = _lambda_value(lp_ref, lam_init)
    for g, (m0, l0, a0, m1, l1, a1) in enumerate(chains):
        out_ref[g * tk:(g + 1) * tk, :] = _diff_finish_t(lam, sub_ref, l0, a0, l1, a1).astype(out_ref.dtype)


def _flash(q0, q1, kb, vb, lam_p, subln, seq, lam_init):
    width = q0.shape[1]
    heads = width // DIFF_DV
    tq = min(FLASH_Q_TILE, seq)
    tk = min(FLASH_K_TILE, tq)
    kv_rows = seq + N_META
    qspec = pl.BlockSpec((tq, DIFF_DV), lambda h, i: (i, h))
    kspec = pl.BlockSpec((kv_rows, DIFF_DV), lambda h, i: (0, h))
    stat = pltpu.VMEM((1, tk), F32)
    acc = pltpu.VMEM((DIFF_DV, tk), F32)
    return pl.pallas_call(
        functools.partial(_flash_kernel, seq=seq, tq=tq, tk=tk, lam_init=lam_init),
        grid=(heads, seq // tq),
        in_specs=[qspec, qspec, kspec, kspec,
                  pl.BlockSpec(lam_p.shape, lambda h, i: (0, 0)), pl.BlockSpec((1, DIFF_DV), lambda h, i: (0, 0))],
        out_specs=pl.BlockSpec((tq, DIFF_DV), lambda h, i: (i, h)),
        out_shape=jax.ShapeDtypeStruct((seq, width), BF16),
        scratch_shapes=[stat, stat, acc, stat, stat, acc] * (tq // tk),
        compiler_params=_params("parallel", "arbitrary"),
        name="diff_attn_prompt",
    )(q0, q1, kb, vb, lam_p, subln)


def _meta_attn_kernel(q0_ref, q1_ref, k_ref, v_ref, lp_ref, sub_ref, out_ref,
                      m0, l0, a0, m1, l1, a1, *, lam_init):
    _init_softmax_state(m0, l0, a0)
    _init_softmax_state(m1, l1, a1)
    t = q0_ref.shape[0]
    ki = lax.broadcasted_iota(jnp.int32, (t, t), 0)
    qj = lax.broadcasted_iota(jnp.int32, (t, t), 1)
    mask = ki <= qj
    _softmax_step_t(q0_ref[...], k_ref[...], v_ref[...], m0, l0, a0, mask)
    _softmax_step_t(q1_ref[...], k_ref[...], v_ref[...], m1, l1, a1, mask)
    out_ref[...] = _diff_finish_t(_lambda_value(lp_ref, lam_init), sub_ref, l0, a0, l1, a1)


def _meta_attn(q0, q1, kb, vb, lam_p, subln, lam_init):
    t, width = q0.shape
    heads = width // DIFF_DV
    spec = pl.BlockSpec((t, DIFF_DV), lambda h: (0, h))
    stat = pltpu.VMEM((1, t), F32)
    acc = pltpu.VMEM((DIFF_DV, t), F32)
    return pl.pallas_call(
        functools.partial(_meta_attn_kernel, lam_init=lam_init),
        grid=(heads,),
        in_specs=[spec, spec, spec, spec,
                  pl.BlockSpec(lam_p.shape, lambda h: (0, 0)), pl.BlockSpec((1, DIFF_DV), lambda h: (0, 0))],
        out_specs=spec,
        out_shape=jax.ShapeDtypeStruct((t, width), F32),
        scratch_shapes=[stat, stat, acc, stat, stat, acc],
        compiler_params=_params("parallel"),
        name="diff_attn_meta",
    )(q0, q1, kb, vb, lam_p, subln)


def _sample_attn_kernel(pt_ref, q_ref, *refs, pages, heads, new_valid, q_len, lam_init):
    k_refs = refs[:pages]
    v_refs = refs[pages:2 * pages]
    kn_ref, vn_ref, lp_ref, sub_ref, out_ref, kcat, vcat, m_s, l_s, acc_s = refs[2 * pages:]
    j = pl.program_id(1)
    page = k_refs[0].shape[1]
    keys = pages * page
    rows_q = q_ref.shape[0]
    group = rows_q // heads

    @pl.when(j == 0)
    def _():
        m_s[...] = jnp.full_like(m_s, MASK_VALUE)
        l_s[...] = jnp.zeros_like(l_s)
        acc_s[...] = jnp.zeros_like(acc_s)

    q = q_ref[...]

    def update(k_t, v_head, mask):
        s = jnp.dot(q, k_t, preferred_element_type=F32)
        if mask is not None:
            s = jnp.where(mask, s, MASK_VALUE)
        m_old = m_s[...]
        m_new = jnp.maximum(m_old, jnp.max(s, axis=1, keepdims=True))
        alpha = jnp.exp2(m_old - m_new)
        p = jnp.exp2(s - m_new)
        l_s[...] = alpha * l_s[...] + jnp.sum(p, axis=1, keepdims=True)
        pv = jnp.concatenate(
            [jnp.dot(p[h * group:(h + 1) * group, :].astype(BF16), v_head(h), preferred_element_type=F32)
             for h in range(heads)], axis=0)
        acc_s[...] = alpha * acc_s[...] + pv
        m_s[...] = m_new

    for i in range(pages):
        kcat[:, i * page:(i + 1) * page] = k_refs[i][...].astype(BF16)
        for h in range(heads):
            vcat[h * keys + i * page:h * keys + (i + 1) * page, :] = (
                v_refs[i][pl.ds(h, page, stride=heads), :].astype(BF16))
    update(kcat[...], lambda h: vcat[h * keys:(h + 1) * keys, :], None)

    @pl.when(j == pl.num_programs(1) - 1)
    def _():
        n_new = kn_ref.shape[1]
        r = lax.broadcasted_iota(jnp.int32, (rows_q, n_new), 0)
        u = lax.broadcasted_iota(jnp.int32, (rows_q, n_new), 1)
        update(kn_ref[...], lambda h: vn_ref[:, h * DIFF_DV:(h + 1) * DIFF_DV], (u <= r % q_len) & (u < new_valid))
        on = acc_s[...] / l_s[...]
        lam = _lambda_value(lp_ref, lam_init)
        d = on - lam * pltpu.roll(on, rows_q - q_len, 0)
        out_ref[...] = d * lax.rsqrt(jnp.mean(d * d, axis=-1, keepdims=True) + EPS) * sub_ref[...]


def _sample_attn(page_table, qbd, cache_kt, cache_v, k_new_t, v_new, lam_p, subln, q_len, lam_init):
    nb, rows_q, width = qbd.shape
    heads = width // DIFF_DV
    page = cache_kt.shape[2]
    n_pages = page_table.shape[1]
    pages = PAGES_PER_STEP
    k_spec = lambda i: pl.BlockSpec((None, width, page), lambda b, j, pt, i=i: (pt[b, j * pages + i], 0, 0))
    v_spec = lambda i: pl.BlockSpec((None, page * heads, DIFF_DV),
                                    lambda b, j, pt, i=i: (pt[b, j * pages + i], 0, 0))
    per_b = lambda shape: pl.BlockSpec((None,) + shape, lambda b, j, pt: (b, 0, 0))
    grid_spec = pltpu.PrefetchScalarGridSpec(
        num_scalar_prefetch=1,
        grid=(nb, n_pages // pages),
        in_specs=[per_b((rows_q, width))] + [k_spec(i) for i in range(pages)] + [v_spec(i) for i in range(pages)]
                 + [per_b(k_new_t.shape[1:]), per_b(v_new.shape[1:]),
                    pl.BlockSpec(lam_p.shape, lambda b, j, pt: (0, 0)),
                    pl.BlockSpec((1, DIFF_DV), lambda b, j, pt: (0, 0))],
        out_specs=pl.BlockSpec((None, rows_q, DIFF_DV), lambda b, j, pt: (b, 0, 0)),
        scratch_shapes=[pltpu.VMEM((width, pages * page), BF16), pltpu.VMEM((heads * pages * page, DIFF_DV), BF16),
                        pltpu.VMEM((rows_q, 1), F32), pltpu.VMEM((rows_q, 1), F32),
                        pltpu.VMEM((rows_q, DIFF_DV), F32)],
    )
    return pl.pallas_call(
        functools.partial(_sample_attn_kernel, pages=pages, heads=heads, new_valid=q_len, q_len=q_len,
                          lam_init=lam_init),
        grid_spec=grid_spec,
        out_shape=jax.ShapeDtypeStruct((nb, rows_q, DIFF_DV), F32),
        compiler_params=_params("parallel", "arbitrary"),
        name="diff_attn_sample",
    )(page_table, qbd, *([cache_kt] * pages), *([cache_v] * pages), k_new_t, v_new, lam_p, subln)


def _rope_tables(pos):
    half = DIFF_DH // 2
    inv = ROPE_THETA ** (-jnp.arange(half, dtype=F32) / half)
    ang = jnp.asarray(pos, F32)[:, None] * inv[None, :]
    cos, sin = jnp.cos(ang), jnp.sin(ang)
    cos = jnp.tile(cos, (1, LANES // half))
    sin = jnp.tile(jnp.concatenate([-sin, sin], axis=1), (1, LANES // DIFF_DH))
    return cos, sin


def _pad_cols(w, n):
    return jnp.pad(w, ((0, 0), (0, n - w.shape[1])))


def kernel(x_prompt, x_sample, cache_k, cache_v, state_C, state_n, state_m, page_table, meta_tokens,
           norm_ffn1, ffn1_w_gate, ffn1_w_up, ffn1_w_down, norm_mix, w_in, mlstm_b_i, mlstm_b_f,
           mlstm_norm, diff_lambda_qk, diff_subln, w_out, norm_ffn2, ffn2_w_gate, ffn2_w_up,
           ffn2_w_down, norm_final):
    batch, seq, d_model = x_prompt.shape
    assert batch == 1 and state_C.shape[0] == 1
    dec_b, dec_t, _ = x_sample.shape
    heads_m = mlstm_b_i.shape[1]
    dk, dv = state_C.shape[3], state_C.shape[4]
    qw, vw = heads_m * dk, heads_m * dv
    dw = w_out.shape[1] - vw
    heads_d = dw // DIFF_DV
    assert vw == 2 * qw and dw == vw, "column blocks of the projection are addressed in units of the q width"
    n_s = dec_b * dec_t
    rows = seq + N_META + n_s
    past = page_table.shape[1] * cache_k.shape[2]
    lam_init = 0.8 - 0.6 * math.exp(-0.0)

    def ffn_weights(wg, wu, wd):
        return wg.astype(BF16), wu.astype(BF16), wd.astype(BF16)

    h = jnp.concatenate([x_prompt[0], meta_tokens.astype(F32), x_sample.reshape(n_s, d_model)], axis=0)

    h = _ffn(h, norm_ffn1[0], *ffn_weights(ffn1_w_gate[0], ffn1_w_up[0], ffn1_w_down[0]))

    wi = w_in[0]
    c_gate = 2 * qw + 2 * vw
    w_main = jnp.concatenate([wi[:, :c_gate], wi[:, c_gate + 2 * heads_m:]], axis=1).astype(BF16)
    w_gate = _pad_cols(wi[:, c_gate:c_gate + 2 * heads_m], LANES).astype(BF16)
    u = _rmsnorm(h, norm_mix[0], BF16, 256)
    tm = _row_tile(rows, 1088)
    z = _mm(u, w_main, tm, 512)
    zg = _mm(u, w_gate, tm, LANES)
    gate_bias = jnp.pad(jnp.concatenate([mlstm_b_i[0], mlstm_b_f[0]]), (0, LANES - 2 * heads_m)).reshape(1, LANES)
    gates = _gates(zg, gate_bias.astype(F32), heads_m)
    log_i, log_f = gates[:, :heads_m], gates[:, heads_m:2 * heads_m]

    def chunk_rows(a):
        a = a.T.reshape(heads_m, seq // CHUNK, CHUNK)
        return jnp.pad(a, ((0, 0), (0, 0), (0, LANES - CHUNK)))

    lane_rows = lambda a: jnp.pad(a, ((0, 0),) * (a.ndim - 1) + ((0, LANES - a.shape[-1]),))
    norm_m = mlstm_norm[0].reshape(1, vw).astype(F32)
    mo_p, mo_meta, c_p, n_p, m_p = _mlstm_prompt(
        z, chunk_rows(log_i[:seq]), chunk_rows(log_f[:seq]),
        lane_rows(log_i[seq:seq + N_META].T), lane_rows(log_f[seq:seq + N_META].T), norm_m, seq, heads_m, dk, dv)

    t_pad = 8
    zs = jnp.pad(z[seq + N_META:, :c_gate].reshape(dec_b, dec_t, c_gate), ((0, 0), (0, t_pad - dec_t), (0, 0)))
    gate_s = lambda a: lane_rows(a[seq + N_META:].reshape(dec_b, dec_t, heads_m).transpose(0, 2, 1))
    m0 = jnp.broadcast_to(state_m[0][:, :, None], (dec_b, heads_m, LANES))
    mo_s, c_s, n_s_new, m_s = _mlstm_sample(zs, gate_s(log_i), gate_s(log_f), state_C[0], state_n[0], m0, norm_m,
                                             heads_m, dk, dv, dec_t)

    pos = np.concatenate([N_META + np.arange(seq), np.arange(N_META), past + np.tile(np.arange(dec_t), dec_b)])
    cos, sin = _rope_tables(pos)
    blk = c_gate // dw
    q0, q1, k_rot, kb, vb = _rope(z, blk, blk + 1, blk + 2, dw, cos, sin)
    sub = (diff_subln[0].astype(F32) * (1.0 - lam_init)).reshape(1, DIFF_DV)
    lam_p = diff_lambda_qk[0].astype(F32)
    do_p = _flash(q0, q1, kb, vb, lam_p, sub, seq, lam_init)

    pad128 = lambda a: jnp.pad(a, ((0, LANES - a.shape[0]), (0, 0)))
    meta = slice(seq, seq + N_META)
    do_meta = _meta_attn(pad128(q0[meta]), pad128(q1[meta]), pad128(kb[meta]), pad128(vb[meta]), lam_p, sub,
                         lam_init)[:N_META]

    qs = (q0[seq + N_META:] + q1[seq + N_META:]).reshape(dec_b, 1, dec_t, dw)
    rows_q = heads_d * 2 * dec_t
    qbd = jnp.broadcast_to(qs, (dec_b, heads_d * 2, dec_t, dw)).reshape(dec_b, rows_q, dw)
    r_id = np.arange(rows_q)[:, None] // dec_t
    c_id = np.arange(dw)[None, :] // DIFF_DH
    qbd = jnp.where(jnp.asarray(r_id == c_id), qbd, jnp.zeros_like(qbd))
    new_rows = lambda a: jnp.pad(a[seq + N_META:].reshape(dec_b, dec_t, dw), ((0, 0), (0, LANES - dec_t), (0, 0)))
    n_phys, page = cache_k.shape[1], cache_k.shape[2]
    cache_kt = jnp.transpose(cache_k[0], (0, 2, 3, 4, 1)).reshape(n_phys, dw, page)
    do_s = _sample_attn(page_table, qbd, cache_kt, cache_v[0].reshape(n_phys, page * heads_d, DIFF_DV),
                        new_rows(kb).transpose(0, 2, 1), new_rows(vb), lam_p, sub, dec_t, lam_init)
    do_s = do_s.reshape(dec_b, heads_d, 2, dec_t, DIFF_DV)[:, :, 0].transpose(0, 2, 1, 3).reshape(n_s, dw)

    mix = jnp.concatenate([
        jnp.concatenate([mo_p, mo_meta, mo_s[:, :dec_t].reshape(n_s, vw).astype(BF16)], axis=0),
        jnp.concatenate([do_p, do_meta.astype(BF16), do_s.astype(BF16)], axis=0)], axis=1)
    h = _mm_resid(mix, w_out[0].astype(BF16), h, 1.0, _row_tile(rows, 1088), 512)

    h = _ffn(h, norm_ffn2[0], *ffn_weights(ffn2_w_gate[0], ffn2_w_up[0], ffn2_w_down[0]))

    y_prompt = _rmsnorm(h, norm_final, F32, 256, 0, seq // 256).reshape(1, seq, d_model)
    y_sample = _rmsnorm(h, norm_final, F32, N_META, (seq + N_META) // N_META, n_s // N_META)
    y_sample = y_sample.reshape(dec_b, dec_t, d_model)

    order = lambda a: jnp.concatenate([a[meta], a[:seq]], axis=0)
    v_all = z[:, (blk + 2) * dw:(blk + 3) * dw]
    k_prompt = order(k_rot).reshape(1, 1, seq + N_META, heads_d, 2, DIFF_DH)
    v_prompt = order(v_all).reshape(1, 1, seq + N_META, heads_d, DIFF_DV)
    k_sample = k_rot[seq + N_META:].reshape(1, dec_b, dec_t, heads_d, 2, DIFF_DH)
    v_sample = v_all[seq + N_META:].reshape(1, dec_b, dec_t, heads_d, DIFF_DV)
    return (y_prompt, y_sample, k_prompt, v_prompt,
            c_p[None, None], n_p[None, None], m_p[None, None, :, 0],
            k_sample, v_sample, c_s[None], n_s_new[None], m_s[None, :, :, 0])
```

```python
import functools
import math

import numpy as np
import jax
import jax.numpy as jnp
from jax import lax
from jax.experimental import pallas as pl
from jax.experimental.pallas import tpu as pltpu

F32 = jnp.float32
BF16 = jnp.bfloat16

N_META = 16
CHUNK = 64
ROPE_THETA = 10000.0
GATE_CAP = 15.0
M_INIT = -1e30
EPS = 1e-6
DIFF_DV = 128
DIFF_DH = DIFF_DV // 2
DIFF_SCALE = DIFF_DH ** -0.5
Q_SCALE_LOG2 = DIFF_SCALE * math.log2(math.e)
PAGES_PER_STEP = 4
FLASH_Q_TILE = 2048
FLASH_K_TILE = 512
MASK_VALUE = -1e30

LANES = 128
V7X_VMEM_BYTES = 64 * 2 ** 20
VMEM_LIMIT = V7X_VMEM_BYTES - 8 * 2 ** 20


def _params(*sem):
    return pltpu.CompilerParams(dimension_semantics=sem, vmem_limit_bytes=VMEM_LIMIT)


def _round_up(x, m):
    return (x + m - 1) // m * m


def _row_tile(rows, target):
    n = pl.cdiv(rows, target)
    return _round_up(pl.cdiv(rows, n), 16)


def _dot(a, b):
    return jnp.dot(a.astype(BF16), b.astype(BF16), preferred_element_type=F32)


def _dot_nt(a, b):
    return lax.dot_general(a.astype(BF16), b.astype(BF16), (((1,), (1,)), ((), ())),
                           preferred_element_type=F32)


def _dot_tn(a, b):
    return lax.dot_general(a.astype(BF16), b.astype(BF16), (((0,), (0,)), ((), ())),
                           preferred_element_type=F32)


def _rmsnorm_kernel(x_ref, g_ref, o_ref):
    x = x_ref[...]
    y = x * lax.rsqrt(jnp.mean(x * x, axis=-1, keepdims=True) + EPS)
    o_ref[...] = (y * g_ref[...]).astype(o_ref.dtype)


def _rmsnorm(x, g, out_dtype, row_block, first_block=0, n_blocks=None):
    rows, d = x.shape
    if n_blocks is None:
        n_blocks, out_rows = pl.cdiv(rows, row_block), rows
    else:
        out_rows = n_blocks * row_block
    return pl.pallas_call(
        _rmsnorm_kernel,
        grid=(n_blocks,),
        in_specs=[pl.BlockSpec((row_block, d), lambda i: (i + first_block, 0)),
                  pl.BlockSpec((1, d), lambda i: (0, 0))],
        out_specs=pl.BlockSpec((row_block, d), lambda i: (i, 0)),
        out_shape=jax.ShapeDtypeStruct((out_rows, d), out_dtype),
        compiler_params=_params("parallel"),
        name="rmsnorm",
    )(x, g.reshape(1, d).astype(F32))


def _gateup_kernel(x_ref, wg_ref, wu_ref, o_ref):
    x = x_ref[...]
    g = jnp.dot(x, wg_ref[...].astype(BF16), preferred_element_type=F32)
    u = jnp.dot(x, wu_ref[...].astype(BF16), preferred_element_type=F32)
    o_ref[...] = (g * jax.nn.sigmoid(g) * u).astype(o_ref.dtype)


def _gateup(x, wg, wu, tm, tn):
    rows, d = x.shape
    n = wg.shape[1]
    return pl.pallas_call(
        _gateup_kernel,
        grid=(pl.cdiv(rows, tm), n // tn),
        in_specs=[pl.BlockSpec((tm, d), lambda i, j: (i, 0)),
                  pl.BlockSpec((d, tn), lambda i, j: (0, j)),
                  pl.BlockSpec((d, tn), lambda i, j: (0, j))],
        out_specs=pl.BlockSpec((tm, tn), lambda i, j: (i, j)),
        out_shape=jax.ShapeDtypeStruct((rows, n), BF16),
        compiler_params=_params("parallel", "arbitrary"),
        name="ffn_gate_up",
    )(x, wg, wu)


def _mm_resid_kernel(a_ref, w_ref, r_ref, o_ref, *, scale):
    acc = jnp.dot(a_ref[...], w_ref[...].astype(BF16), preferred_element_type=F32)
    o_ref[...] = r_ref[...] + scale * acc


def _mm_resid(a, w, resid, scale, tm, tn):
    rows, k = a.shape
    n = w.shape[1]
    return pl.pallas_call(
        functools.partial(_mm_resid_kernel, scale=scale),
        grid=(pl.cdiv(rows, tm), n // tn),
        in_specs=[pl.BlockSpec((tm, k), lambda i, j: (i, 0)),
                  pl.BlockSpec((k, tn), lambda i, j: (0, j)),
                  pl.BlockSpec((tm, tn), lambda i, j: (i, j))],
        out_specs=pl.BlockSpec((tm, tn), lambda i, j: (i, j)),
        out_shape=jax.ShapeDtypeStruct((rows, n), F32),
        compiler_params=_params("parallel", "arbitrary"),
        name="matmul_residual",
    )(a, w, resid)


def _mm_kernel(a_ref, w_ref, o_ref):
    o_ref[...] = jnp.dot(a_ref[...], w_ref[...], preferred_element_type=F32)


def _mm(a, w, tm, tn):
    rows, k = a.shape
    n = w.shape[1]
    return pl.pallas_call(
        _mm_kernel,
        grid=(pl.cdiv(rows, tm), n // tn),
        in_specs=[pl.BlockSpec((tm, k), lambda i, j: (i, 0)),
                  pl.BlockSpec((k, tn), lambda i, j: (0, j))],
        out_specs=pl.BlockSpec((tm, tn), lambda i, j: (i, j)),
        out_shape=jax.ShapeDtypeStruct((rows, n), F32),
        compiler_params=_params("parallel", "arbitrary"),
        name="matmul",
    )(a, w)


def _ffn(h, g, wg, wu, wd):
    rows = h.shape[0]
    xn = _rmsnorm(h, g, BF16, 256)
    hid = _gateup(xn, wg, wu, _row_tile(rows, 1088), 512 if wg.shape[1] % 512 == 0 else 256)
    return _mm_resid(hid, wd, h, 0.5, _row_tile(rows, 544), 256)


def _gate_kernel(z_ref, b_ref, o_ref, *, heads):
    z = z_ref[...] + b_ref[...]
    c = GATE_CAP * jnp.tanh(z / GATE_CAP)
    log_sig = jnp.minimum(c, 0.0) - jnp.log1p(jnp.exp(-jnp.abs(c)))
    lane = lax.broadcasted_iota(jnp.int32, c.shape, 1)
    o_ref[...] = jnp.where(lane < heads, c, log_sig)


def _gates(z, bias, heads):
    rows, d = z.shape
    rb = 512
    return pl.pallas_call(
        functools.partial(_gate_kernel, heads=heads),
        grid=(pl.cdiv(rows, rb),),
        in_specs=[pl.BlockSpec((rb, d), lambda i: (i, 0)), pl.BlockSpec((1, d), lambda i: (0, 0))],
        out_specs=pl.BlockSpec((rb, d), lambda i: (i, 0)),
        out_shape=jax.ShapeDtypeStruct((rows, d), F32),
        compiler_params=_params("parallel"),
        name="mlstm_gates",
    )(z, bias)


def _rope_kernel(q_ref, k_ref, v_ref, cos_ref, sin_ref, q0_ref, q1_ref, kf_ref, kb_ref, vb_ref):
    cos = cos_ref[...]
    sin = sin_ref[...]
    lane = lax.broadcasted_iota(jnp.int32, cos.shape, 1)
    first_half = (lane % DIFF_DH) < (DIFF_DH // 2)
    map0 = lane < DIFF_DH

    def rot(x):
        partner = jnp.where(first_half, pltpu.roll(x, LANES - DIFF_DH // 2, 1), pltpu.roll(x, DIFF_DH // 2, 1))
        return x * cos + partner * sin

    for j in range(q_ref.shape[1] // LANES):
        sl = slice(j * LANES, (j + 1) * LANES)
        q = rot(q_ref[:, sl]) * Q_SCALE_LOG2
        q0_ref[:, sl] = jnp.where(map0, q, 0.0).astype(BF16)
        q1_ref[:, sl] = jnp.where(map0, 0.0, q).astype(BF16)
        k = rot(k_ref[:, sl])
        kf_ref[:, sl] = k
        kb_ref[:, sl] = k.astype(BF16)
    vb_ref[...] = v_ref[...].astype(BF16)


def _rope(z, col_q, col_k, col_v, width, cos, sin):
    rows = z.shape[0]
    rb = 256
    zspec = lambda c: pl.BlockSpec((rb, width), lambda i, c=c: (i, c))
    tspec = pl.BlockSpec((rb, LANES), lambda i: (i, 0))
    ospec = pl.BlockSpec((rb, width), lambda i: (i, 0))
    bshape = jax.ShapeDtypeStruct((rows, width), BF16)
    return pl.pallas_call(
        _rope_kernel,
        grid=(pl.cdiv(rows, rb),),
        in_specs=[zspec(col_q), zspec(col_k), zspec(col_v), tspec, tspec],
        out_specs=[ospec] * 5,
        out_shape=[bshape, bshape, jax.ShapeDtypeStruct((rows, width), F32), bshape, bshape],
        compiler_params=_params("parallel"),
        name="rope",
    )(z, z, z, cos, sin)


def _lane_cumsum(x):
    lane = lax.broadcasted_iota(jnp.int32, x.shape, 1)
    k = 1
    while k < x.shape[1]:
        x = x + jnp.where(lane >= k, pltpu.roll(x, k, 1), 0.0)
        k *= 2
    return x


def _mlstm_heads(load, store, c_ref, n_ref, m_ref, heads, valid):
    ins = [load(h) for h in range(heads)]
    t, dk = ins[0][0].shape
    ri = lax.broadcasted_iota(jnp.int32, (t, t), 0)
    ci = lax.broadcasted_iota(jnp.int32, (t, t), 1)
    eye = ri == ci
    lane = lax.broadcasted_iota(jnp.int32, (1, t), 1)

    stage1 = []
    for h, (q, k, v, o, li_row, b_row, norm_row) in enumerate(ins):
        k = k * (dk ** -0.5)
        c_state = c_ref[h]
        stage1.append((k, c_state, _dot_nt(q, k), _dot(q, c_state)))

    stage2 = []
    for h, (q, k_raw, v, o, li_row, b_row, norm_row) in enumerate(ins):
        k, c_state, qk, qc = stage1[h]
        m_prev = m_ref[h:h + 1, 0:1]
        b_mat = jnp.broadcast_to(b_row, (t, t))
        b_col = jnp.sum(jnp.where(eye, b_mat, 0.0), axis=1, keepdims=True)
        log_d = jnp.where(ci <= ri, b_col - b_mat + li_row, -jnp.inf)
        m_inter = m_prev + b_col
        m_t = jnp.maximum(m_inter, jnp.max(log_d, axis=1, keepdims=True))
        w_inter = jnp.exp(m_inter - m_t)
        s = qk * jnp.exp(log_d - m_t)
        m_new = m_t[valid - 1:valid, :]
        decay = jnp.exp(m_inter[valid - 1:valid, :] - m_new)
        wk_row = jnp.where(lane < valid, jnp.exp(b_row[:, valid - 1:valid] - b_row + li_row - m_new), 0.0)
        wk_col = jnp.sum(jnp.where(eye, jnp.broadcast_to(wk_row, (t, t)), 0.0), axis=1, keepdims=True)
        stage2.append((s, k * wk_col, w_inter, m_t, m_new, decay))

    stage3 = []
    for h, (q, k_raw, v, o, li_row, b_row, norm_row) in enumerate(ins):
        s, kw = stage2[h][:2]
        stage3.append((_dot(s, v), _dot_tn(kw, v)))

    for h, (q, k_raw, v, o, li_row, b_row, norm_row) in enumerate(ins):
        k, c_state, qk, qc = stage1[h]
        s, kw, w_inter, m_t, m_new, decay = stage2[h]
        sv, kv = stage3[h]
        n_row = n_ref[h:h + 1, :]
        num = w_inter * qc + sv
        den = w_inter * jnp.sum(q * n_row, axis=1, keepdims=True) + jnp.sum(s, axis=1, keepdims=True)
        y = num / jnp.maximum(jnp.abs(den), jnp.exp(-m_t))
        c_ref[h] = decay * c_state + kv
        n_ref[h:h + 1, :] = decay * n_row + jnp.sum(kw, axis=0, keepdims=True)
        m_ref[h:h + 1, :] = jnp.broadcast_to(m_new, (1, m_ref.shape[1]))
        yn = y * lax.rsqrt(jnp.mean(y * y, axis=-1, keepdims=True) + EPS) * norm_row
        store(h, jax.nn.sigmoid(o) * yn)


def _mlstm_prompt_kernel(q_ref, k_ref, v_ref, o_ref, li_ref, lf_ref,
                         qm_ref, km_ref, vm_ref, om_ref, lim_ref, lfm_ref, norm_ref,
                         out_ref, outm_ref, c_ref, n_ref, m_ref, *, heads, dk, dv, chunks):
    g = pl.program_id(0)

    @pl.when(g == 0)
    def _():
        c_ref[...] = jnp.zeros_like(c_ref)
        n_ref[...] = jnp.zeros_like(n_ref)
        m_ref[...] = jnp.full_like(m_ref, M_INIT)
        bm = _lane_cumsum(lfm_ref[...])

        def load_meta(h):
            return (qm_ref[:, h * dk:(h + 1) * dk], km_ref[:, h * dk:(h + 1) * dk],
                    vm_ref[:, h * dv:(h + 1) * dv], om_ref[:, h * dv:(h + 1) * dv],
                    lim_ref[h:h + 1, 0:N_META], bm[h:h + 1, 0:N_META], norm_ref[:, h * dv:(h + 1) * dv])

        def store_meta(h, y):
            outm_ref[:, h * dv:(h + 1) * dv] = y.astype(outm_ref.dtype)

        _mlstm_heads(load_meta, store_meta, c_ref, n_ref, m_ref, heads, N_META)

    def body(c, carry):
        r0 = pl.multiple_of(c * CHUNK, CHUNK)
        rows = pl.ds(r0, CHUNK)
        gc = g * chunks + c

        def load(h):
            return (q_ref[rows, h * dk:(h + 1) * dk], k_ref[rows, h * dk:(h + 1) * dk],
                    v_ref[rows, h * dv:(h + 1) * dv], o_ref[rows, h * dv:(h + 1) * dv],
                    li_ref[h, pl.ds(gc, 1), :][:, 0:CHUNK], _lane_cumsum(lf_ref[h, pl.ds(gc, 1), :])[:, 0:CHUNK],
                    norm_ref[:, h * dv:(h + 1) * dv])

        def store(h, y):
            out_ref[rows, h * dv:(h + 1) * dv] = y.astype(out_ref.dtype)

        _mlstm_heads(load, store, c_ref, n_ref, m_ref, heads, CHUNK)
        return carry

    lax.fori_loop(0, chunks, body, 0)


def _mlstm_prompt(z, li, lf, li_meta, lf_meta, norm, seq, heads, dk, dv, mix_width):
    chunks = 4
    rb = chunks * CHUNK
    qw, vw = heads * dk, heads * dv
    meta_blk = seq // N_META
    spec = lambda w, c: pl.BlockSpec((rb, w), lambda g, c=c: (g, c))
    mspec = lambda w, c: pl.BlockSpec((N_META, w), lambda g, c=c: (meta_blk, c))
    gspec = pl.BlockSpec((heads, seq // CHUNK, LANES), lambda g: (0, 0, 0))
    const2 = lambda shape: pl.BlockSpec(shape, lambda g: (0,) * len(shape))
    return pl.pallas_call(
        functools.partial(_mlstm_prompt_kernel, heads=heads, dk=dk, dv=dv, chunks=chunks),
        grid=(seq // rb,),
        in_specs=[spec(qw, 0), spec(qw, 1), spec(vw, 1), spec(vw, 2), gspec, gspec,
                  mspec(qw, 0), mspec(qw, 1), mspec(vw, 1), mspec(vw, 2),
                  const2((heads, LANES)), const2((heads, LANES)), const2((1, vw))],
        out_specs=[pl.BlockSpec((rb, vw), lambda g: (g, 0)), const2((N_META, vw)),
                   const2((heads, dk, dv)), const2((heads, dk)), const2((heads, LANES))],
        out_shape=[jax.ShapeDtypeStruct((z.shape[0], mix_width), BF16), jax.ShapeDtypeStruct((N_META, vw), BF16),
                   jax.ShapeDtypeStruct((heads, dk, dv), F32), jax.ShapeDtypeStruct((heads, dk), F32),
                   jax.ShapeDtypeStruct((heads, LANES), F32)],
        compiler_params=_params("arbitrary"),
        name="mlstm_prompt",
    )(z, z, z, z, li, lf, z, z, z, z, li_meta, lf_meta, norm)


def _mlstm_sample_kernel(q_ref, k_ref, v_ref, o_ref, li_ref, lf_ref, c0_ref, n0_ref, m0_ref, norm_ref,
                         out_ref, c_ref, n_ref, m_ref, *, heads, dk, dv, valid):
    c_ref[...] = c0_ref[...]
    n_ref[...] = n0_ref[...]
    m_ref[...] = m0_ref[...]
    t = q_ref.shape[0]
    b_all = _lane_cumsum(lf_ref[...])

    def load(h):
        return (q_ref[:, h * dk:(h + 1) * dk], k_ref[:, h * dk:(h + 1) * dk],
                v_ref[:, h * dv:(h + 1) * dv], o_ref[:, h * dv:(h + 1) * dv],
                li_ref[h:h + 1, 0:t], b_all[h:h + 1, 0:t], norm_ref[:, h * dv:(h + 1) * dv])

    def store(h, y):
        out_ref[:, h * dv:(h + 1) * dv] = y

    _mlstm_heads(load, store, c_ref, n_ref, m_ref, heads, valid)


def _mlstm_sample(zs, li, lf, c0, n0, m0, norm, heads, dk, dv, valid):
    nb, t, _ = zs.shape
    qw, vw = heads * dk, heads * dv
    spec = lambda w, c: pl.BlockSpec((None, t, w), lambda b, c=c: (b, 0, c))
    per_b = lambda *shape: pl.BlockSpec((None,) + shape, lambda b: (b,) + (0,) * len(shape))
    return pl.pallas_call(
        functools.partial(_mlstm_sample_kernel, heads=heads, dk=dk, dv=dv, valid=valid),
        grid=(nb,),
        in_specs=[spec(qw, 0), spec(qw, 1), spec(vw, 1), spec(vw, 2), per_b(heads, LANES), per_b(heads, LANES),
                  per_b(heads, dk, dv), per_b(heads, dk), per_b(heads, LANES),
                  pl.BlockSpec((1, vw), lambda b: (0, 0))],
        out_specs=[per_b(t, vw), per_b(heads, dk, dv), per_b(heads, dk), per_b(heads, LANES)],
        out_shape=[jax.ShapeDtypeStruct((nb, t, vw), F32), jax.ShapeDtypeStruct((nb, heads, dk, dv), F32),
                   jax.ShapeDtypeStruct((nb, heads, dk), F32), jax.ShapeDtypeStruct((nb, heads, LANES), F32)],
        compiler_params=_params("parallel"),
        name="mlstm_sample",
    )(zs, zs, zs, zs, li, lf, c0, n0, m0, norm)


def _lambda_value(lp_ref, lam_init):
    lp = lp_ref[...]
    a = jnp.sum(lp[0:1, :] * lp[1:2, :], axis=1, keepdims=True)
    b = jnp.sum(lp[2:3, :] * lp[3:4, :], axis=1, keepdims=True)
    return jnp.exp(a) - jnp.exp(b) + lam_init


def _scores_t(q, kb, mask):
    s = _dot_nt(kb, q)
    return s if mask is None else jnp.where(mask, s, MASK_VALUE)


def _softmax_step_t(q, kb, vb, m_ref, l_ref, acc_ref, mask):
    _softmax_update_t(_scores_t(q, kb, mask), vb, m_ref, l_ref, acc_ref)


def _softmax_update_t(s, vb, m_ref, l_ref, acc_ref, cols=slice(None)):
    m_old = m_ref[:, cols]
    m_new = jnp.maximum(m_old, jnp.max(s, axis=0, keepdims=True))
    alpha = jnp.exp2(m_old - m_new)
    p = jnp.exp2(s - m_new)
    l_ref[:, cols] = alpha * l_ref[:, cols] + jnp.sum(p, axis=0, keepdims=True)
    acc_ref[:, cols] = alpha * acc_ref[:, cols] + _dot_tn(vb, p)
    m_ref[:, cols] = m_new


def _diff_finish_t(lam, sub_ref, l0_ref, a0_ref, l1_ref, a1_ref):
    o_t = a0_ref[...] / l0_ref[...] - lam * (a1_ref[...] / l1_ref[...])
    o = o_t.T
    return o * lax.rsqrt(jnp.mean(o * o, axis=-1, keepdims=True) + EPS) * sub_ref[...]


def _init_softmax_state(m_ref, l_ref, acc_ref):
    m_ref[...] = jnp.full_like(m_ref, MASK_VALUE)
    l_ref[...] = jnp.zeros_like(l_ref)
    acc_ref[...] = jnp.zeros_like(acc_ref)


def _flash_kernel(q0_ref, q1_ref, k_ref, v_ref, lp_ref, sub_ref, mix_ref, out_ref, *state, seq, tq, tk, lam_init):
    del mix_ref
    qi = pl.program_id(1)
    groups = tq // tk
    chains = [state[6 * g:6 * g + 6] for g in range(groups)]
    km, vm = k_ref[seq:seq + N_META, :], v_ref[seq:seq + N_META, :]
    queries = []
    for g, (m0, l0, a0, m1, l1, a1) in enumerate(chains):
        q0 = q0_ref[g * tk:(g + 1) * tk, :]
        q1 = q1_ref[g * tk:(g + 1) * tk, :]
        queries.append((q0, q1))
        _init_softmax_state(m0, l0, a0)
        _init_softmax_state(m1, l1, a1)
        _softmax_step_t(q0, km, vm, m0, l0, a0, None)
        _softmax_step_t(q1, km, vm, m1, l1, a1, None)

    ki = lax.broadcasted_iota(jnp.int32, (tk, tk), 0)
    qj = lax.broadcasted_iota(jnp.int32, (tk, tk), 1)
    causal = ki <= qj

    def step(j, first_group, diagonal):
        rows = pl.ds(pl.multiple_of(j * tk, tk), tk)
        kb, vb = k_ref[rows, :], v_ref[rows, :]
        scores = []
        for g in range(first_group, groups):
            mask = causal if (diagonal and g == first_group) else None
            scores.append((_scores_t(queries[g][0], kb, mask), _scores_t(queries[g][1], kb, mask)))
        for g, (s0, s1) in zip(range(first_group, groups), scores):
            m0, l0, a0, m1, l1, a1 = chains[g]
            _softmax_update_t(s0, vb, m0, l0, a0)
            _softmax_update_t(s1, vb, m1, l1, a1)

    def body(j, carry):
        step(j, 0, False)
        return carry

    lax.fori_loop(0, qi * groups, body, 0)
    for g in range(groups):
        step(qi * groups + g, g, True)

    lam = _lambda_value(lp_ref, lam_init)
    for g, (m0, l0, a0, m1, l1, a1) in enumerate(chains):
        out_ref[g * tk:(g + 1) * tk, :] = _diff_finish_t(lam, sub_ref, l0, a0, l1, a1).astype(out_ref.dtype)


def _flash(q0, q1, kb, vb, lam_p, subln, mix, seq, lam_init):
    width = q0.shape[1]
    heads = width // DIFF_DV
    col0 = (mix.shape[1] - width) // DIFF_DV
    tq = min(FLASH_Q_TILE, seq)
    tk = min(FLASH_K_TILE, tq)
    kv_rows = seq + N_META
    qspec = pl.BlockSpec((tq, DIFF_DV), lambda h, i: (i, h))
    kspec = pl.BlockSpec((kv_rows, DIFF_DV), lambda h, i: (0, h))
    stat = pltpu.VMEM((1, tk), F32)
    acc = pltpu.VMEM((DIFF_DV, tk), F32)
    return pl.pallas_call(
        functools.partial(_flash_kernel, seq=seq, tq=tq, tk=tk, lam_init=lam_init),
        grid=(heads, seq // tq),
        in_specs=[qspec, qspec, kspec, kspec,
                  pl.BlockSpec(lam_p.shape, lambda h, i: (0, 0)), pl.BlockSpec((1, DIFF_DV), lambda h, i: (0, 0)),
                  pl.BlockSpec(memory_space=pl.ANY)],
        out_specs=pl.BlockSpec((tq, DIFF_DV), lambda h, i: (i, col0 + h)),
        out_shape=jax.ShapeDtypeStruct(mix.shape, mix.dtype),
        input_output_aliases={6: 0},
        scratch_shapes=[stat, stat, acc, stat, stat, acc] * (tq // tk),
        compiler_params=_params("parallel", "arbitrary"),
        name="diff_attn_prompt",
    )(q0, q1, kb, vb, lam_p, subln, mix)


def _meta_attn_kernel(q0_ref, q1_ref, k_ref, v_ref, lp_ref, sub_ref, out_ref,
                      m0, l0, a0, m1, l1, a1, *, lam_init):
    _init_softmax_state(m0, l0, a0)
    _init_softmax_state(m1, l1, a1)
    t = q0_ref.shape[0]
    ki = lax.broadcasted_iota(jnp.int32, (t, t), 0)
    qj = lax.broadcasted_iota(jnp.int32, (t, t), 1)
    mask = ki <= qj
    _softmax_step_t(q0_ref[...], k_ref[...], v_ref[...], m0, l0, a0, mask)
    _softmax_step_t(q1_ref[...], k_ref[...], v_ref[...], m1, l1, a1, mask)
    out_ref[...] = _diff_finish_t(_lambda_value(lp_ref, lam_init), sub_ref, l0, a0, l1, a1)


def _meta_attn(q0, q1, kb, vb, lam_p, subln, lam_init):
    t, width = q0.shape
    heads = width // DIFF_DV
    spec = pl.BlockSpec((t, DIFF_DV), lambda h: (0, h))
    stat = pltpu.VMEM((1, t), F32)
    acc = pltpu.VMEM((DIFF_DV, t), F32)
    return pl.pallas_call(
        functools.partial(_meta_attn_kernel, lam_init=lam_init),
        grid=(heads,),
        in_specs=[spec, spec, spec, spec,
                  pl.BlockSpec(lam_p.shape, lambda h: (0, 0)), pl.BlockSpec((1, DIFF_DV), lambda h: (0, 0))],
        out_specs=spec,
        out_shape=jax.ShapeDtypeStruct((t, width), F32),
        scratch_shapes=[stat, stat, acc, stat, stat, acc],
        compiler_params=_params("parallel"),
        name="diff_attn_meta",
    )(q0, q1, kb, vb, lam_p, subln)


def _sample_attn_kernel(pt_ref, q_ref, *refs, pages, heads, new_valid, q_len, lam_init):
    k_refs = refs[:pages]
    v_refs = refs[pages:2 * pages]
    kn_ref, vn_ref, lp_ref, sub_ref, out_ref, kcat, vcat, m_s, l_s, acc_s = refs[2 * pages:]
    j = pl.program_id(1)
    page = k_refs[0].shape[1]
    keys = pages * page
    rows_q = q_ref.shape[0]
    group = rows_q // heads

    @pl.when(j == 0)
    def _():
        m_s[...] = jnp.full_like(m_s, MASK_VALUE)
        l_s[...] = jnp.zeros_like(l_s)
        acc_s[...] = jnp.zeros_like(acc_s)

    q = q_ref[...]

    def update(k_t, v_head, mask):
        s = jnp.dot(q, k_t, preferred_element_type=F32)
        if mask is not None:
            s = jnp.where(mask, s, MASK_VALUE)
        m_old = m_s[...]
        m_new = jnp.maximum(m_old, jnp.max(s, axis=1, keepdims=True))
        alpha = jnp.exp2(m_old - m_new)
        p = jnp.exp2(s - m_new)
        l_s[...] = alpha * l_s[...] + jnp.sum(p, axis=1, keepdims=True)
        pv = jnp.concatenate(
            [jnp.dot(p[h * group:(h + 1) * group, :].astype(BF16), v_head(h), preferred_element_type=F32)
             for h in range(heads)], axis=0)
        acc_s[...] = alpha * acc_s[...] + pv
        m_s[...] = m_new

    for i in range(pages):
        kcat[:, i * page:(i + 1) * page] = k_refs[i][...].astype(BF16)
        for h in range(heads):
            vcat[h * keys + i * page:h * keys + (i + 1) * page, :] = (
                v_refs[i][pl.ds(h, page, stride=heads), :].astype(BF16))
    update(kcat[...], lambda h: vcat[h * keys:(h + 1) * keys, :], None)

    @pl.when(j == pl.num_programs(1) - 1)
    def _():
        n_new = kn_ref.shape[1]
        r = lax.broadcasted_iota(jnp.int32, (rows_q, n_new), 0)
        u = lax.broadcasted_iota(jnp.int32, (rows_q, n_new), 1)
        update(kn_ref[...], lambda h: vn_ref[:, h * DIFF_DV:(h + 1) * DIFF_DV], (u <= r % q_len) & (u < new_valid))
        on = acc_s[...] / l_s[...]
        lam = _lambda_value(lp_ref, lam_init)
        d = on - lam * pltpu.roll(on, rows_q - q_len, 0)
        out_ref[...] = d * lax.rsqrt(jnp.mean(d * d, axis=-1, keepdims=True) + EPS) * sub_ref[...]


def _sample_attn(page_table, qbd, cache_kt, cache_v, k_new_t, v_new, lam_p, subln, q_len, lam_init):
    nb, rows_q, width = qbd.shape
    heads = width // DIFF_DV
    page = cache_kt.shape[2]
    n_pages = page_table.shape[1]
    pages = PAGES_PER_STEP
    k_spec = lambda i: pl.BlockSpec((None, width, page), lambda b, j, pt, i=i: (pt[b, j * pages + i], 0, 0))
    v_spec = lambda i: pl.BlockSpec((None, page * heads, DIFF_DV),
                                    lambda b, j, pt, i=i: (pt[b, j * pages + i], 0, 0))
    per_b = lambda shape: pl.BlockSpec((None,) + shape, lambda b, j, pt: (b, 0, 0))
    grid_spec = pltpu.PrefetchScalarGridSpec(
        num_scalar_prefetch=1,
        grid=(nb, n_pages // pages),
        in_specs=[per_b((rows_q, width))] + [k_spec(i) for i in range(pages)] + [v_spec(i) for i in range(pages)]
                 + [per_b(k_new_t.shape[1:]), per_b(v_new.shape[1:]),
                    pl.BlockSpec(lam_p.shape, lambda b, j, pt: (0, 0)),
                    pl.BlockSpec((1, DIFF_DV), lambda b, j, pt: (0, 0))],
        out_specs=pl.BlockSpec((None, rows_q, DIFF_DV), lambda b, j, pt: (b, 0, 0)),
        scratch_shapes=[pltpu.VMEM((width, pages * page), BF16), pltpu.VMEM((heads * pages * page, DIFF_DV), BF16),
                        pltpu.VMEM((rows_q, 1), F32), pltpu.VMEM((rows_q, 1), F32),
                        pltpu.VMEM((rows_q, DIFF_DV), F32)],
    )
    return pl.pallas_call(
        functools.partial(_sample_attn_kernel, pages=pages, heads=heads, new_valid=q_len, q_len=q_len,
                          lam_init=lam_init),
        grid_spec=grid_spec,
        out_shape=jax.ShapeDtypeStruct((nb, rows_q, DIFF_DV), F32),
        compiler_params=_params("parallel", "arbitrary"),
        name="diff_attn_sample",
    )(page_table, qbd, *([cache_kt] * pages), *([cache_v] * pages), k_new_t, v_new, lam_p, subln)


def _rope_tables(pos):
    half = DIFF_DH // 2
    inv = ROPE_THETA ** (-jnp.arange(half, dtype=F32) / half)
    ang = jnp.asarray(pos, F32)[:, None] * inv[None, :]
    cos, sin = jnp.cos(ang), jnp.sin(ang)
    cos = jnp.tile(cos, (1, LANES // half))
    sin = jnp.tile(jnp.concatenate([-sin, sin], axis=1), (1, LANES // DIFF_DH))
    return cos, sin


def _pad_cols(w, n):
    return jnp.pad(w, ((0, 0), (0, n - w.shape[1])))


def kernel(x_prompt, x_sample, cache_k, cache_v, state_C, state_n, state_m, page_table, meta_tokens,
           norm_ffn1, ffn1_w_gate, ffn1_w_up, ffn1_w_down, norm_mix, w_in, mlstm_b_i, mlstm_b_f,
           mlstm_norm, diff_lambda_qk, diff_subln, w_out, norm_ffn2, ffn2_w_gate, ffn2_w_up,
           ffn2_w_down, norm_final):
    batch, seq, d_model = x_prompt.shape
    assert batch == 1 and state_C.shape[0] == 1
    dec_b, dec_t, _ = x_sample.shape
    heads_m = mlstm_b_i.shape[1]
    dk, dv = state_C.shape[3], state_C.shape[4]
    qw, vw = heads_m * dk, heads_m * dv
    dw = w_out.shape[1] - vw
    heads_d = dw // DIFF_DV
    assert vw == 2 * qw and dw == vw, "column blocks of the projection are addressed in units of the q width"
    n_s = dec_b * dec_t
    rows = seq + N_META + n_s
    past = page_table.shape[1] * cache_k.shape[2]
    lam_init = 0.8 - 0.6 * math.exp(-0.0)

    def ffn_weights(wg, wu, wd):
        return wg, wu, wd.astype(BF16)

    h = jnp.concatenate([x_prompt[0], meta_tokens.astype(F32), x_sample.reshape(n_s, d_model)], axis=0)

    h = _ffn(h, norm_ffn1[0], *ffn_weights(ffn1_w_gate[0], ffn1_w_up[0], ffn1_w_down[0]))

    wi = w_in[0]
    c_gate = 2 * qw + 2 * vw
    w_main = jnp.concatenate([wi[:, :c_gate], wi[:, c_gate + 2 * heads_m:]], axis=1).astype(BF16)
    w_gate = _pad_cols(wi[:, c_gate:c_gate + 2 * heads_m], LANES).astype(BF16)
    u = _rmsnorm(h, norm_mix[0], BF16, 256)
    tm = _row_tile(rows, 1088)
    z = _mm(u, w_main, tm, 512)
    zg = _mm(u, w_gate, tm, LANES)
    gate_bias = jnp.pad(jnp.concatenate([mlstm_b_i[0], mlstm_b_f[0]]), (0, LANES - 2 * heads_m)).reshape(1, LANES)
    gates = _gates(zg, gate_bias.astype(F32), heads_m)
    log_i, log_f = gates[:, :heads_m], gates[:, heads_m:2 * heads_m]

    def chunk_rows(a):
        a = a.T.reshape(heads_m, seq // CHUNK, CHUNK)
        return jnp.pad(a, ((0, 0), (0, 0), (0, LANES - CHUNK)))

    lane_rows = lambda a: jnp.pad(a, ((0, 0),) * (a.ndim - 1) + ((0, LANES - a.shape[-1]),))
    norm_m = mlstm_norm[0].reshape(1, vw).astype(F32)
    mix, mo_meta, c_p, n_p, m_p = _mlstm_prompt(
        z, chunk_rows(log_i[:seq]), chunk_rows(log_f[:seq]),
        lane_rows(log_i[seq:seq + N_META].T), lane_rows(log_f[seq:seq + N_META].T), norm_m, seq, heads_m, dk, dv,
        vw + dw)

    t_pad = 8
    zs = jnp.pad(z[seq + N_META:, :c_gate].reshape(dec_b, dec_t, c_gate), ((0, 0), (0, t_pad - dec_t), (0, 0)))
    gate_s = lambda a: lane_rows(a[seq + N_META:].reshape(dec_b, dec_t, heads_m).transpose(0, 2, 1))
    m0 = jnp.broadcast_to(state_m[0][:, :, None], (dec_b, heads_m, LANES))
    mo_s, c_s, n_s_new, m_s = _mlstm_sample(zs, gate_s(log_i), gate_s(log_f), state_C[0], state_n[0], m0, norm_m,
                                             heads_m, dk, dv, dec_t)

    pos = np.concatenate([N_META + np.arange(seq), np.arange(N_META), past + np.tile(np.arange(dec_t), dec_b)])
    cos, sin = _rope_tables(pos)
    blk = c_gate // dw
    q0, q1, k_rot, kb, vb = _rope(z, blk, blk + 1, blk + 2, dw, cos, sin)
    sub = (diff_subln[0].astype(F32) * (1.0 - lam_init)).reshape(1, DIFF_DV)
    lam_p = diff_lambda_qk[0].astype(F32)
    mix = _flash(q0, q1, kb, vb, lam_p, sub, mix, seq, lam_init)

    pad128 = lambda a: jnp.pad(a, ((0, LANES - a.shape[0]), (0, 0)))
    meta = slice(seq, seq + N_META)
    do_meta = _meta_attn(pad128(q0[meta]), pad128(q1[meta]), pad128(kb[meta]), pad128(vb[meta]), lam_p, sub,
                         lam_init)[:N_META]

    qs = (q0[seq + N_META:] + q1[seq + N_META:]).reshape(dec_b, 1, dec_t, dw)
    rows_q = heads_d * 2 * dec_t
    qbd = jnp.broadcast_to(qs, (dec_b, heads_d * 2, dec_t, dw)).reshape(dec_b, rows_q, dw)
    r_id = np.arange(rows_q)[:, None] // dec_t
    c_id = np.arange(dw)[None, :] // DIFF_DH
    qbd = jnp.where(jnp.asarray(r_id == c_id), qbd, jnp.zeros_like(qbd))
    new_rows = lambda a: jnp.pad(a[seq + N_META:].reshape(dec_b, dec_t, dw), ((0, 0), (0, LANES - dec_t), (0, 0)))
    n_phys, page = cache_k.shape[1], cache_k.shape[2]
    cache_kt = jnp.transpose(cache_k[0], (0, 2, 3, 4, 1)).reshape(n_phys, dw, page)
    do_s = _sample_attn(page_table, qbd, cache_kt, cache_v[0].reshape(n_phys, page * heads_d, DIFF_DV),
                        new_rows(kb).transpose(0, 2, 1), new_rows(vb), lam_p, sub, dec_t, lam_init)
    do_s = do_s.reshape(dec_b, heads_d, 2, dec_t, DIFF_DV)[:, :, 0].transpose(0, 2, 1, 3).reshape(n_s, dw)

    tail = jnp.concatenate([
        jnp.concatenate([mo_meta, mo_s[:, :dec_t].reshape(n_s, vw).astype(BF16)], axis=0),
        jnp.concatenate([do_meta.astype(BF16), do_s.astype(BF16)], axis=0)], axis=1)
    mix = lax.dynamic_update_slice(mix, tail, (seq, 0))
    h = _mm_resid(mix, w_out[0], h, 1.0, _row_tile(rows, 1088), 512)

    h = _ffn(h, norm_ffn2[0], *ffn_weights(ffn2_w_gate[0], ffn2_w_up[0], ffn2_w_down[0]))

    y_prompt = _rmsnorm(h, norm_final, F32, 256, 0, seq // 256).reshape(1, seq, d_model)
    y_sample = _rmsnorm(h, norm_final, F32, N_META, (seq + N_META) // N_META, n_s // N_META)
    y_sample = y_sample.reshape(dec_b, dec_t, d_model)

    order = lambda a: jnp.concatenate([a[meta], a[:seq]], axis=0)
    v_all = z[:, (blk + 2) * dw:(blk + 3) * dw]
    k_prompt = order(k_rot).reshape(1, 1, seq + N_META, heads_d, 2, DIFF_DH)
    v_prompt = order(v_all).reshape(1, 1, seq + N_META, heads_d, DIFF_DV)
    k_sample = k_rot[seq + N_META:].reshape(1, dec_b, dec_t, heads_d, 2, DIFF_DH)
    v_sample = v_all[seq + N_META:].reshape(1, dec_b, dec_t, heads_d, DIFF_DV)
    return (y_prompt, y_sample, k_prompt, v_prompt,
            c_p[None, None], n_p[None, None], m_p[None, None, :, 0],
            k_sample, v_sample, c_s[None], n_s_new[None], m_s[None, :, :, 0])
```

```python
import functools
import math

import numpy as np
import jax
import jax.numpy as jnp
from jax import lax
from jax.experimental import pallas as pl
from jax.experimental.pallas import tpu as pltpu

F32 = jnp.float32
BF16 = jnp.bfloat16

N_META = 16
CHUNK = 64
ROPE_THETA = 10000.0
GATE_CAP = 15.0
M_INIT = -1e30
EPS = 1e-6
DIFF_DV = 128
DIFF_DH = DIFF_DV // 2
DIFF_SCALE = DIFF_DH ** -0.5
Q_SCALE_LOG2 = DIFF_SCALE * math.log2(math.e)
PAGES_PER_STEP = 4
FLASH_Q_TILE = 2048
FLASH_K_TILE = 512
MASK_VALUE = -1e30

LANES = 128
V7X_VMEM_BYTES = 64 * 2 ** 20
VMEM_LIMIT = V7X_VMEM_BYTES - 8 * 2 ** 20


def _params(*sem):
    return pltpu.CompilerParams(dimension_semantics=sem, vmem_limit_bytes=VMEM_LIMIT)


def _round_up(x, m):
    return (x + m - 1) // m * m


def _row_tile(rows, target):
    n = pl.cdiv(rows, target)
    return _round_up(pl.cdiv(rows, n), 16)


def _col_tile(n):
    return 512 if n % 512 == 0 else 256


def _dot(a, b):
    return jnp.dot(a.astype(BF16), b.astype(BF16), preferred_element_type=F32)


def _dot_nt(a, b):
    return lax.dot_general(a.astype(BF16), b.astype(BF16), (((1,), (1,)), ((), ())),
                           preferred_element_type=F32)


def _dot_tn(a, b):
    return lax.dot_general(a.astype(BF16), b.astype(BF16), (((0,), (0,)), ((), ())),
                           preferred_element_type=F32)


def _rmsnorm_kernel(x_ref, g_ref, o_ref):
    x = x_ref[...]
    y = x * lax.rsqrt(jnp.mean(x * x, axis=-1, keepdims=True) + EPS)
    o_ref[...] = (y * g_ref[...]).astype(o_ref.dtype)


def _rmsnorm(x, g, out_dtype, row_block, first_block=0, n_blocks=None):
    rows, d = x.shape
    if n_blocks is None:
        n_blocks, out_rows = pl.cdiv(rows, row_block), rows
    else:
        out_rows = n_blocks * row_block
    return pl.pallas_call(
        _rmsnorm_kernel,
        grid=(n_blocks,),
        in_specs=[pl.BlockSpec((row_block, d), lambda i: (i + first_block, 0)),
                  pl.BlockSpec((1, d), lambda i: (0, 0))],
        out_specs=pl.BlockSpec((row_block, d), lambda i: (i, 0)),
        out_shape=jax.ShapeDtypeStruct((out_rows, d), out_dtype),
        compiler_params=_params("parallel"),
        name="rmsnorm",
    )(x, g.reshape(1, d).astype(F32))


def _gateup_kernel(x_ref, wg_ref, wu_ref, o_ref):
    x = x_ref[...]
    g = jnp.dot(x, wg_ref[...].astype(BF16), preferred_element_type=F32)
    u = jnp.dot(x, wu_ref[...].astype(BF16), preferred_element_type=F32)
    o_ref[...] = (g * jax.nn.sigmoid(g) * u).astype(o_ref.dtype)


def _gateup(x, wg, wu, tm, tn):
    rows, d = x.shape
    n = wg.shape[1]
    return pl.pallas_call(
        _gateup_kernel,
        grid=(pl.cdiv(rows, tm), n // tn),
        in_specs=[pl.BlockSpec((tm, d), lambda i, j: (i, 0)),
                  pl.BlockSpec((d, tn), lambda i, j: (0, j)),
                  pl.BlockSpec((d, tn), lambda i, j: (0, j))],
        out_specs=pl.BlockSpec((tm, tn), lambda i, j: (i, j)),
        out_shape=jax.ShapeDtypeStruct((rows, n), BF16),
        compiler_params=_params("parallel", "arbitrary"),
        name="ffn_gate_up",
    )(x, wg, wu)


def _mm_resid_kernel(a_ref, w_ref, r_ref, o_ref, *, scale):
    acc = jnp.dot(a_ref[...], w_ref[...].astype(BF16), preferred_element_type=F32)
    o_ref[...] = r_ref[...] + scale * acc


def _mm_resid(a, w, resid, scale, tm, tn):
    rows, k = a.shape
    n = w.shape[1]
    return pl.pallas_call(
        functools.partial(_mm_resid_kernel, scale=scale),
        grid=(pl.cdiv(rows, tm), n // tn),
        in_specs=[pl.BlockSpec((tm, k), lambda i, j: (i, 0)),
                  pl.BlockSpec((k, tn), lambda i, j: (0, j)),
                  pl.BlockSpec((tm, tn), lambda i, j: (i, j))],
        out_specs=pl.BlockSpec((tm, tn), lambda i, j: (i, j)),
        out_shape=jax.ShapeDtypeStruct((rows, n), F32),
        compiler_params=_params("parallel", "arbitrary"),
        name="matmul_residual",
    )(a, w, resid)


def _mm_kernel(a_ref, w_ref, o_ref):
    o_ref[...] = jnp.dot(a_ref[...], w_ref[...].astype(BF16), preferred_element_type=F32)


def _mm(a, w, tm, tn, n=None):
    rows, k = a.shape
    n = w.shape[1] if n is None else n
    return pl.pallas_call(
        _mm_kernel,
        grid=(pl.cdiv(rows, tm), n // tn),
        in_specs=[pl.BlockSpec((tm, k), lambda i, j: (i, 0)),
                  pl.BlockSpec((k, tn), lambda i, j: (0, j))],
        out_specs=pl.BlockSpec((tm, tn), lambda i, j: (i, j)),
        out_shape=jax.ShapeDtypeStruct((rows, n), F32),
        compiler_params=_params("parallel", "arbitrary"),
        name="matmul",
    )(a, w)


def _ffn(h, g, wg, wu, wd):
    rows = h.shape[0]
    xn = _rmsnorm(h, g, BF16, 256)
    hid = _gateup(xn, wg, wu, _row_tile(rows, 1088), _col_tile(wg.shape[1]))
    return _mm_resid(hid, wd, h, 0.5, _row_tile(rows, 544), 512)


def _gate_kernel(z_ref, b_ref, o_ref, *, heads):
    z = z_ref[...] + b_ref[...]
    c = GATE_CAP * jnp.tanh(z / GATE_CAP)
    log_sig = jnp.minimum(c, 0.0) - jnp.log1p(jnp.exp(-jnp.abs(c)))
    lane = lax.broadcasted_iota(jnp.int32, c.shape, 1)
    o_ref[...] = jnp.where(lane < heads, c, log_sig)


def _gates(z, bias, heads):
    rows, d = z.shape
    rb = 512
    return pl.pallas_call(
        functools.partial(_gate_kernel, heads=heads),
        grid=(pl.cdiv(rows, rb),),
        in_specs=[pl.BlockSpec((rb, d), lambda i: (i, 0)), pl.BlockSpec((1, d), lambda i: (0, 0))],
        out_specs=pl.BlockSpec((rb, d), lambda i: (i, 0)),
        out_shape=jax.ShapeDtypeStruct((rows, d), F32),
        compiler_params=_params("parallel"),
        name="mlstm_gates",
    )(z, bias)


def _rope_kernel(q_ref, k_ref, v_ref, cos_ref, sin_ref, q0_ref, q1_ref, kf_ref, kb_ref, vb_ref):
    cos = cos_ref[...]
    sin = sin_ref[...]
    lane = lax.broadcasted_iota(jnp.int32, cos.shape, 1)
    first_half = (lane % DIFF_DH) < (DIFF_DH // 2)
    map0 = lane < DIFF_DH

    def rot(x):
        partner = jnp.where(first_half, pltpu.roll(x, LANES - DIFF_DH // 2, 1), pltpu.roll(x, DIFF_DH // 2, 1))
        return x * cos + partner * sin

    for j in range(q_ref.shape[1] // LANES):
        sl = slice(j * LANES, (j + 1) * LANES)
        q = rot(q_ref[:, sl]) * Q_SCALE_LOG2
        q0_ref[:, sl] = jnp.where(map0, q, 0.0).astype(BF16)
        q1_ref[:, sl] = jnp.where(map0, 0.0, q).astype(BF16)
        k = rot(k_ref[:, sl])
        kf_ref[:, sl] = k
        kb_ref[:, sl] = k.astype(BF16)
    vb_ref[...] = v_ref[...].astype(BF16)


def _rope(z, col_q, col_k, col_v, width, cos, sin):
    rows = z.shape[0]
    rb = 256
    zspec = lambda c: pl.BlockSpec((rb, width), lambda i, c=c: (i, c))
    tspec = pl.BlockSpec((rb, LANES), lambda i: (i, 0))
    ospec = pl.BlockSpec((rb, width), lambda i: (i, 0))
    bshape = jax.ShapeDtypeStruct((rows, width), BF16)
    return pl.pallas_call(
        _rope_kernel,
        grid=(pl.cdiv(rows, rb),),
        in_specs=[zspec(col_q), zspec(col_k), zspec(col_v), tspec, tspec],
        out_specs=[ospec] * 5,
        out_shape=[bshape, bshape, jax.ShapeDtypeStruct((rows, width), F32), bshape, bshape],
        compiler_params=_params("parallel"),
        name="rope",
    )(z, z, z, cos, sin)


def _lane_cumsum(x):
    lane = lax.broadcasted_iota(jnp.int32, x.shape, 1)
    k = 1
    while k < x.shape[1]:
        x = x + jnp.where(lane >= k, pltpu.roll(x, k, 1), 0.0)
        k *= 2
    return x


def _mlstm_heads(load, store, c_ref, n_ref, m_ref, heads, valid):
    ins = [load(h) for h in range(heads)]
    t, dk = ins[0][0].shape
    ri = lax.broadcasted_iota(jnp.int32, (t, t), 0)
    ci = lax.broadcasted_iota(jnp.int32, (t, t), 1)
    eye = ri == ci
    lane = lax.broadcasted_iota(jnp.int32, (1, t), 1)

    stage1 = []
    for h, (q, k, v, o, li_row, b_row, norm_row) in enumerate(ins):
        k = k * (dk ** -0.5)
        c_state = c_ref[h]
        stage1.append((k, c_state, _dot_nt(q, k), _dot(q, c_state)))

    stage2 = []
    for h, (q, k_raw, v, o, li_row, b_row, norm_row) in enumerate(ins):
        k, c_state, qk, qc = stage1[h]
        m_prev = m_ref[h:h + 1, 0:1]
        b_mat = jnp.broadcast_to(b_row, (t, t))
        b_col = jnp.sum(jnp.where(eye, b_mat, 0.0), axis=1, keepdims=True)
        log_d = jnp.where(ci <= ri, b_col - b_mat + li_row, -jnp.inf)
        m_inter = m_prev + b_col
        m_t = jnp.maximum(m_inter, jnp.max(log_d, axis=1, keepdims=True))
        w_inter = jnp.exp(m_inter - m_t)
        s = qk * jnp.exp(log_d - m_t)
        m_new = m_t[valid - 1:valid, :]
        decay = jnp.exp(m_inter[valid - 1:valid, :] - m_new)
        wk_row = jnp.where(lane < valid, jnp.exp(b_row[:, valid - 1:valid] - b_row + li_row - m_new), 0.0)
        wk_col = jnp.sum(jnp.where(eye, jnp.broadcast_to(wk_row, (t, t)), 0.0), axis=1, keepdims=True)
        stage2.append((s, k * wk_col, w_inter, m_t, m_new, decay))

    stage3 = []
    for h, (q, k_raw, v, o, li_row, b_row, norm_row) in enumerate(ins):
        s, kw = stage2[h][:2]
        stage3.append((_dot(s, v), _dot_tn(kw, v)))

    for h, (q, k_raw, v, o, li_row, b_row, norm_row) in enumerate(ins):
        k, c_state, qk, qc = stage1[h]
        s, kw, w_inter, m_t, m_new, decay = stage2[h]
        sv, kv = stage3[h]
        n_row = n_ref[h:h + 1, :]
        num = w_inter * qc + sv
        den = w_inter * jnp.sum(q * n_row, axis=1, keepdims=True) + jnp.sum(s, axis=1, keepdims=True)
        y = num / jnp.maximum(jnp.abs(den), jnp.exp(-m_t))
        c_ref[h] = decay * c_state + kv
        n_ref[h:h + 1, :] = decay * n_row + jnp.sum(kw, axis=0, keepdims=True)
        m_ref[h:h + 1, :] = jnp.broadcast_to(m_new, (1, m_ref.shape[1]))
        yn = y * lax.rsqrt(jnp.mean(y * y, axis=-1, keepdims=True) + EPS) * norm_row
        store(h, jax.nn.sigmoid(o) * yn)


def _mlstm_prompt_kernel(q_ref, k_ref, v_ref, o_ref, li_ref, lf_ref,
                         qm_ref, km_ref, vm_ref, om_ref, lim_ref, lfm_ref, norm_ref,
                         out_ref, outm_ref, c_ref, n_ref, m_ref, *, heads, dk, dv, chunks):
    g = pl.program_id(0)

    @pl.when(g == 0)
    def _():
        c_ref[...] = jnp.zeros_like(c_ref)
        n_ref[...] = jnp.zeros_like(n_ref)
        m_ref[...] = jnp.full_like(m_ref, M_INIT)
        bm = _lane_cumsum(lfm_ref[...])

        def load_meta(h):
            return (qm_ref[:, h * dk:(h + 1) * dk], km_ref[:, h * dk:(h + 1) * dk],
                    vm_ref[:, h * dv:(h + 1) * dv], om_ref[:, h * dv:(h + 1) * dv],
                    lim_ref[h:h + 1, 0:N_META], bm[h:h + 1, 0:N_META], norm_ref[:, h * dv:(h + 1) * dv])

        def store_meta(h, y):
            outm_ref[:, h * dv:(h + 1) * dv] = y.astype(outm_ref.dtype)

        _mlstm_heads(load_meta, store_meta, c_ref, n_ref, m_ref, heads, N_META)

    def body(c, carry):
        r0 = pl.multiple_of(c * CHUNK, CHUNK)
        rows = pl.ds(r0, CHUNK)
        gc = g * chunks + c

        def load(h):
            return (q_ref[rows, h * dk:(h + 1) * dk], k_ref[rows, h * dk:(h + 1) * dk],
                    v_ref[rows, h * dv:(h + 1) * dv], o_ref[rows, h * dv:(h + 1) * dv],
                    li_ref[h, pl.ds(gc, 1), :][:, 0:CHUNK], _lane_cumsum(lf_ref[h, pl.ds(gc, 1), :])[:, 0:CHUNK],
                    norm_ref[:, h * dv:(h + 1) * dv])

        def store(h, y):
            out_ref[rows, h * dv:(h + 1) * dv] = y.astype(out_ref.dtype)

        _mlstm_heads(load, store, c_ref, n_ref, m_ref, heads, CHUNK)
        return carry

    lax.fori_loop(0, chunks, body, 0)


def _mlstm_prompt(z, li, lf, li_meta, lf_meta, norm, seq, heads, dk, dv, mix_width):
    chunks = 4
    rb = chunks * CHUNK
    qw, vw = heads * dk, heads * dv
    meta_blk = seq // N_META
    spec = lambda w, c: pl.BlockSpec((rb, w), lambda g, c=c: (g, c))
    mspec = lambda w, c: pl.BlockSpec((N_META, w), lambda g, c=c: (meta_blk, c))
    gspec = pl.BlockSpec((heads, seq // CHUNK, LANES), lambda g: (0, 0, 0))
    const2 = lambda shape: pl.BlockSpec(shape, lambda g: (0,) * len(shape))
    return pl.pallas_call(
        functools.partial(_mlstm_prompt_kernel, heads=heads, dk=dk, dv=dv, chunks=chunks),
        grid=(seq // rb,),
        in_specs=[spec(qw, 0), spec(qw, 1), spec(vw, 1), spec(vw, 2), gspec, gspec,
                  mspec(qw, 0), mspec(qw, 1), mspec(vw, 1), mspec(vw, 2),
                  const2((heads, LANES)), const2((heads, LANES)), const2((1, vw))],
        out_specs=[pl.BlockSpec((rb, vw), lambda g: (g, 0)), const2((N_META, vw)),
                   const2((heads, dk, dv)), const2((heads, dk)), const2((heads, LANES))],
        out_shape=[jax.ShapeDtypeStruct((z.shape[0], mix_width), BF16), jax.ShapeDtypeStruct((N_META, vw), BF16),
                   jax.ShapeDtypeStruct((heads, dk, dv), F32), jax.ShapeDtypeStruct((heads, dk), F32),
                   jax.ShapeDtypeStruct((heads, LANES), F32)],
        compiler_params=_params("arbitrary"),
        name="mlstm_prompt",
    )(z, z, z, z, li, lf, z, z, z, z, li_meta, lf_meta, norm)


def _mlstm_sample_kernel(q_ref, k_ref, v_ref, o_ref, li_ref, lf_ref, c0_ref, n0_ref, m0_ref, norm_ref,
                         out_ref, c_ref, n_ref, m_ref, *, heads, dk, dv, valid):
    c_ref[...] = c0_ref[...]
    n_ref[...] = n0_ref[...]
    m_ref[...] = m0_ref[...]
    t = q_ref.shape[0]
    b_all = _lane_cumsum(lf_ref[...])

    def load(h):
        return (q_ref[:, h * dk:(h + 1) * dk], k_ref[:, h * dk:(h + 1) * dk],
                v_ref[:, h * dv:(h + 1) * dv], o_ref[:, h * dv:(h + 1) * dv],
                li_ref[h:h + 1, 0:t], b_all[h:h + 1, 0:t], norm_ref[:, h * dv:(h + 1) * dv])

    def store(h, y):
        out_ref[:, h * dv:(h + 1) * dv] = y

    _mlstm_heads(load, store, c_ref, n_ref, m_ref, heads, valid)


def _mlstm_sample(zs, li, lf, c0, n0, m0, norm, heads, dk, dv, valid):
    nb, t, _ = zs.shape
    qw, vw = heads * dk, heads * dv
    spec = lambda w, c: pl.BlockSpec((None, t, w), lambda b, c=c: (b, 0, c))
    per_b = lambda *shape: pl.BlockSpec((None,) + shape, lambda b: (b,) + (0,) * len(shape))
    return pl.pallas_call(
        functools.partial(_mlstm_sample_kernel, heads=heads, dk=dk, dv=dv, valid=valid),
        grid=(nb,),
        in_specs=[spec(qw, 0), spec(qw, 1), spec(vw, 1), spec(vw, 2), per_b(heads, LANES), per_b(heads, LANES),
                  per_b(heads, dk, dv), per_b(heads, dk), per_b(heads, LANES),
                  pl.BlockSpec((1, vw), lambda b: (0, 0))],
        out_specs=[per_b(t, vw), per_b(heads, dk, dv), per_b(heads, dk), per_b(heads, LANES)],
        out_shape=[jax.ShapeDtypeStruct((nb, t, vw), F32), jax.ShapeDtypeStruct((nb, heads, dk, dv), F32),
                   jax.ShapeDtypeStruct((nb, heads, dk), F32), jax.ShapeDtypeStruct((nb, heads, LANES), F32)],
        compiler_params=_params("parallel"),
        name="mlstm_sample",
    )(zs, zs, zs, zs, li, lf, c0, n0, m0, norm)


def _lambda_value(lp_ref, lam_init):
    lp = lp_ref[...]
    a = jnp.sum(lp[0:1, :] * lp[1:2, :], axis=1, keepdims=True)
    b = jnp.sum(lp[2:3, :] * lp[3:4, :], axis=1, keepdims=True)
    return jnp.exp(a) - jnp.exp(b) + lam_init


def _scores_t(q, kb, mask):
    s = _dot_nt(kb, q)
    return s if mask is None else jnp.where(mask, s, MASK_VALUE)


def _softmax_step_t(q, kb, vb, m_ref, l_ref, acc_ref, mask):
    _softmax_update_t(_scores_t(q, kb, mask), vb, m_ref, l_ref, acc_ref)


def _softmax_update_t(s, vb, m_ref, l_ref, acc_ref, cols=slice(None)):
    m_old = m_ref[:, cols]
    m_new = jnp.maximum(m_old, jnp.max(s, axis=0, keepdims=True))
    alpha = jnp.exp2(m_old - m_new)
    p = jnp.exp2(s - m_new)
    l_ref[:, cols] = alpha * l_ref[:, cols] + jnp.sum(p, axis=0, keepdims=True)
    acc_ref[:, cols] = alpha * acc_ref[:, cols] + _dot_tn(vb, p)
    m_ref[:, cols] = m_new


def _diff_finish_t(lam, sub_ref, l0_ref, a0_ref, l1_ref, a1_ref):
    o_t = a0_ref[...] / l0_ref[...] - lam * (a1_ref[...] / l1_ref[...])
    o = o_t.T
    return o * lax.rsqrt(jnp.mean(o * o, axis=-1, keepdims=True) + EPS) * sub_ref[...]


def _init_softmax_state(m_ref, l_ref, acc_ref):
    m_ref[...] = jnp.full_like(m_ref, MASK_VALUE)
    l_ref[...] = jnp.zeros_like(l_ref)
    acc_ref[...] = jnp.zeros_like(acc_ref)


def _flash_kernel(q0_ref, q1_ref, k_ref, v_ref, lp_ref, sub_ref, mix_ref, out_ref, *state, seq, tq, tk, lam_init):
    del mix_ref
    qi = pl.program_id(1)
    groups = tq // tk
    chains = [state[6 * g:6 * g + 6] for g in range(groups)]
    km, vm = k_ref[seq:seq + N_META, :], v_ref[seq:seq + N_META, :]
    queries = []
    for g, (m0, l0, a0, m1, l1, a1) in enumerate(chains):
        q0 = q0_ref[g * tk:(g + 1) * tk, :]
        q1 = q1_ref[g * tk:(g + 1) * tk, :]
        queries.append((q0, q1))
        _init_softmax_state(m0, l0, a0)
        _init_softmax_state(m1, l1, a1)
        _softmax_step_t(q0, km, vm, m0, l0, a0, None)
        _softmax_step_t(q1, km, vm, m1, l1, a1, None)

    ki = lax.broadcasted_iota(jnp.int32, (tk, tk), 0)
    qj = lax.broadcasted_iota(jnp.int32, (tk, tk), 1)
    causal = ki <= qj

    def step(j, first_group, diagonal):
        rows = pl.ds(pl.multiple_of(j * tk, tk), tk)
        kb, vb = k_ref[rows, :], v_ref[rows, :]
        scores = []
        for g in range(first_group, groups):
            mask = causal if (diagonal and g == first_group) else None
            scores.append((_scores_t(queries[g][0], kb, mask), _scores_t(queries[g][1], kb, mask)))
        for g, (s0, s1) in zip(range(first_group, groups), scores):
            m0, l0, a0, m1, l1, a1 = chains[g]
            _softmax_update_t(s0, vb, m0, l0, a0)
            _softmax_update_t(s1, vb, m1, l1, a1)

    def body(j, carry):
        step(j, 0, False)
        return carry

    lax.fori_loop(0, qi * groups, body, 0)
    for g in range(groups):
        step(qi * groups + g, g, True)

    lam = _lambda_value(lp_ref, lam_init)
    for g, (m0, l0, a0, m1, l1, a1) in enumerate(chains):
        out_ref[g * tk:(g + 1) * tk, :] = _diff_finish_t(lam, sub_ref, l0, a0, l1, a1).astype(out_ref.dtype)


def _flash(q0, q1, kb, vb, lam_p, subln, mix, seq, lam_init):
    width = q0.shape[1]
    heads = width // DIFF_DV
    col0 = (mix.shape[1] - width) // DIFF_DV
    tq = min(FLASH_Q_TILE, seq)
    tk = min(FLASH_K_TILE, tq)
    kv_rows = seq + N_META
    qspec = pl.BlockSpec((tq, DIFF_DV), lambda h, i: (i, h))
    kspec = pl.BlockSpec((kv_rows, DIFF_DV), lambda h, i: (0, h))
    stat = pltpu.VMEM((1, tk), F32)
    acc = pltpu.VMEM((DIFF_DV, tk), F32)
    return pl.pallas_call(
        functools.partial(_flash_kernel, seq=seq, tq=tq, tk=tk, lam_init=lam_init),
        grid=(heads, seq // tq),
        in_specs=[qspec, qspec, kspec, kspec,
                  pl.BlockSpec(lam_p.shape, lambda h, i: (0, 0)), pl.BlockSpec((1, DIFF_DV), lambda h, i: (0, 0)),
                  pl.BlockSpec(memory_space=pl.ANY)],
        out_specs=pl.BlockSpec((tq, DIFF_DV), lambda h, i: (i, col0 + h)),
        out_shape=jax.ShapeDtypeStruct(mix.shape, mix.dtype),
        input_output_aliases={6: 0},
        scratch_shapes=[stat, stat, acc, stat, stat, acc] * (tq // tk),
        compiler_params=_params("parallel", "arbitrary"),
        name="diff_attn_prompt",
    )(q0, q1, kb, vb, lam_p, subln, mix)


def _meta_attn_kernel(q0_ref, q1_ref, k_ref, v_ref, lp_ref, sub_ref, out_ref,
                      m0, l0, a0, m1, l1, a1, *, lam_init):
    _init_softmax_state(m0, l0, a0)
    _init_softmax_state(m1, l1, a1)
    t = q0_ref.shape[0]
    ki = lax.broadcasted_iota(jnp.int32, (t, t), 0)
    qj = lax.broadcasted_iota(jnp.int32, (t, t), 1)
    mask = ki <= qj
    _softmax_step_t(q0_ref[...], k_ref[...], v_ref[...], m0, l0, a0, mask)
    _softmax_step_t(q1_ref[...], k_ref[...], v_ref[...], m1, l1, a1, mask)
    out_ref[...] = _diff_finish_t(_lambda_value(lp_ref, lam_init), sub_ref, l0, a0, l1, a1)


def _meta_attn(q0, q1, kb, vb, lam_p, subln, lam_init):
    t, width = q0.shape
    heads = width // DIFF_DV
    spec = pl.BlockSpec((t, DIFF_DV), lambda h: (0, h))
    stat = pltpu.VMEM((1, t), F32)
    acc = pltpu.VMEM((DIFF_DV, t), F32)
    return pl.pallas_call(
        functools.partial(_meta_attn_kernel, lam_init=lam_init),
        grid=(heads,),
        in_specs=[spec, spec, spec, spec,
                  pl.BlockSpec(lam_p.shape, lambda h: (0, 0)), pl.BlockSpec((1, DIFF_DV), lambda h: (0, 0))],
        out_specs=spec,
        out_shape=jax.ShapeDtypeStruct((t, width), F32),
        scratch_shapes=[stat, stat, acc, stat, stat, acc],
        compiler_params=_params("parallel"),
        name="diff_attn_meta",
    )(q0, q1, kb, vb, lam_p, subln)


def _sample_attn_kernel(pt_ref, q_ref, *refs, pages, heads, new_valid, q_len, lam_init):
    k_refs = refs[:pages]
    v_refs = refs[pages:2 * pages]
    kn_ref, vn_ref, lp_ref, sub_ref, out_ref, kcat, vcat, m_s, l_s, acc_s = refs[2 * pages:]
    j = pl.program_id(1)
    page = k_refs[0].shape[1]
    keys = pages * page
    rows_q = q_ref.shape[0]
    group = rows_q // heads

    @pl.when(j == 0)
    def _():
        m_s[...] = jnp.full_like(m_s, MASK_VALUE)
        l_s[...] = jnp.zeros_like(l_s)
        acc_s[...] = jnp.zeros_like(acc_s)

    q = q_ref[...]

    def update(k_t, weighted_values, mask):
        s = jnp.dot(q, k_t, preferred_element_type=F32)
        if mask is not None:
            s = jnp.where(mask, s, MASK_VALUE)
        m_old = m_s[...]
        m_new = jnp.maximum(m_old, jnp.max(s, axis=1, keepdims=True))
        alpha = jnp.exp2(m_old - m_new)
        p = jnp.exp2(s - m_new)
        l_s[...] = alpha * l_s[...] + jnp.sum(p, axis=1, keepdims=True)
        acc_s[...] = alpha * acc_s[...] + weighted_values(p)
        m_s[...] = m_new

    def cache_values(p):
        own_head = (lax.broadcasted_iota(jnp.int32, (heads, rows_q), 0)
                    == lax.broadcasted_iota(jnp.int32, (heads, rows_q), 1) // group)
        p_t = p.T
        expanded = jnp.where(own_head[None], p_t[:, None, :], 0.0).reshape(keys * heads, rows_q)
        return _dot_tn(expanded, vcat[...])

    def new_values(p):
        return jnp.concatenate(
            [jnp.dot(p[h * group:(h + 1) * group, :].astype(BF16), vn_ref[:, h * DIFF_DV:(h + 1) * DIFF_DV],
                     preferred_element_type=F32) for h in range(heads)], axis=0)

    for i in range(pages):
        kcat[:, i * page:(i + 1) * page] = k_refs[i][...].astype(BF16)
        vcat[i * page * heads:(i + 1) * page * heads, :] = v_refs[i][...].astype(BF16)
    update(kcat[...], cache_values, None)

    @pl.when(j == pl.num_programs(1) - 1)
    def _():
        n_new = kn_ref.shape[1]
        r = lax.broadcasted_iota(jnp.int32, (rows_q, n_new), 0)
        u = lax.broadcasted_iota(jnp.int32, (rows_q, n_new), 1)
        update(kn_ref[...], new_values, (u <= r % q_len) & (u < new_valid))
        on = acc_s[...] / l_s[...]
        lam = _lambda_value(lp_ref, lam_init)
        d = on - lam * pltpu.roll(on, rows_q - q_len, 0)
        out_ref[...] = d * lax.rsqrt(jnp.mean(d * d, axis=-1, keepdims=True) + EPS) * sub_ref[...]


def _sample_attn(page_table, qbd, cache_kt, cache_v, k_new_t, v_new, lam_p, subln, q_len, lam_init):
    nb, rows_q, width = qbd.shape
    heads = width // DIFF_DV
    page = cache_kt.shape[2]
    n_pages = page_table.shape[1]
    pages = PAGES_PER_STEP
    k_spec = lambda i: pl.BlockSpec((None, width, page), lambda b, j, pt, i=i: (pt[b, j * pages + i], 0, 0))
    v_spec = lambda i: pl.BlockSpec((None, page * heads, DIFF_DV),
                                    lambda b, j, pt, i=i: (pt[b, j * pages + i], 0, 0))
    per_b = lambda shape: pl.BlockSpec((None,) + shape, lambda b, j, pt: (b, 0, 0))
    grid_spec = pltpu.PrefetchScalarGridSpec(
        num_scalar_prefetch=1,
        grid=(nb, n_pages // pages),
        in_specs=[per_b((rows_q, width))] + [k_spec(i) for i in range(pages)] + [v_spec(i) for i in range(pages)]
                 + [per_b(k_new_t.shape[1:]), per_b(v_new.shape[1:]),
                    pl.BlockSpec(lam_p.shape, lambda b, j, pt: (0, 0)),
                    pl.BlockSpec((1, DIFF_DV), lambda b, j, pt: (0, 0))],
        out_specs=pl.BlockSpec((None, rows_q, DIFF_DV), lambda b, j, pt: (b, 0, 0)),
        scratch_shapes=[pltpu.VMEM((width, pages * page), BF16), pltpu.VMEM((heads * pages * page, DIFF_DV), BF16),
                        pltpu.VMEM((rows_q, 1), F32), pltpu.VMEM((rows_q, 1), F32),
                        pltpu.VMEM((rows_q, DIFF_DV), F32)],
    )
    return pl.pallas_call(
        functools.partial(_sample_attn_kernel, pages=pages, heads=heads, new_valid=q_len, q_len=q_len,
                          lam_init=lam_init),
        grid_spec=grid_spec,
        out_shape=jax.ShapeDtypeStruct((nb, rows_q, DIFF_DV), F32),
        compiler_params=_params("parallel", "arbitrary"),
        name="diff_attn_sample",
    )(page_table, qbd, *([cache_kt] * pages), *([cache_v] * pages), k_new_t, v_new, lam_p, subln)


def _rope_tables(pos):
    half = DIFF_DH // 2
    inv = ROPE_THETA ** (-jnp.arange(half, dtype=F32) / half)
    ang = jnp.asarray(pos, F32)[:, None] * inv[None, :]
    cos, sin = jnp.cos(ang), jnp.sin(ang)
    cos = jnp.tile(cos, (1, LANES // half))
    sin = jnp.tile(jnp.concatenate([-sin, sin], axis=1), (1, LANES // DIFF_DH))
    return cos, sin


def _pad_cols(w, n):
    return jnp.pad(w, ((0, 0), (0, n - w.shape[1])))


def kernel(x_prompt, x_sample, cache_k, cache_v, state_C, state_n, state_m, page_table, meta_tokens,
           norm_ffn1, ffn1_w_gate, ffn1_w_up, ffn1_w_down, norm_mix, w_in, mlstm_b_i, mlstm_b_f,
           mlstm_norm, diff_lambda_qk, diff_subln, w_out, norm_ffn2, ffn2_w_gate, ffn2_w_up,
           ffn2_w_down, norm_final):
    batch, seq, d_model = x_prompt.shape
    assert batch == 1 and state_C.shape[0] == 1
    dec_b, dec_t, _ = x_sample.shape
    heads_m = mlstm_b_i.shape[1]
    dk, dv = state_C.shape[3], state_C.shape[4]
    qw, vw = heads_m * dk, heads_m * dv
    dw = w_out.shape[1] - vw
    heads_d = dw // DIFF_DV
    assert vw == 2 * qw and dw == vw, "column blocks of the projection are addressed in units of the q width"
    n_s = dec_b * dec_t
    rows = seq + N_META + n_s
    past = page_table.shape[1] * cache_k.shape[2]
    lam_init = 0.8 - 0.6 * math.exp(-0.0)

    def ffn_weights(wg, wu, wd):
        return wg, wu, wd.astype(BF16)

    h = jnp.concatenate([x_prompt[0], meta_tokens.astype(F32), x_sample.reshape(n_s, d_model)], axis=0)

    h = _ffn(h, norm_ffn1[0], *ffn_weights(ffn1_w_gate[0], ffn1_w_up[0], ffn1_w_down[0]))

    wi = w_in[0]
    c_gate = 2 * qw + 2 * vw
    w_gate = _pad_cols(wi[:, c_gate:c_gate + 2 * heads_m], LANES)
    u = _rmsnorm(h, norm_mix[0], BF16, 256)
    tm = _row_tile(rows, 1088)
    z = _mm(u, wi, tm, _col_tile(c_gate), c_gate)
    zd = _mm(u, wi[:, c_gate + 2 * heads_m:], tm, _col_tile(3 * dw))
    zg = _mm(u, w_gate, tm, LANES)
    gate_bias = jnp.pad(jnp.concatenate([mlstm_b_i[0], mlstm_b_f[0]]), (0, LANES - 2 * heads_m)).reshape(1, LANES)
    gates = _gates(zg, gate_bias.astype(F32), heads_m)
    log_i, log_f = gates[:, :heads_m], gates[:, heads_m:2 * heads_m]

    def chunk_rows(a):
        a = a.T.reshape(heads_m, seq // CHUNK, CHUNK)
        return jnp.pad(a, ((0, 0), (0, 0), (0, LANES - CHUNK)))

    lane_rows = lambda a: jnp.pad(a, ((0, 0),) * (a.ndim - 1) + ((0, LANES - a.shape[-1]),))
    norm_m = mlstm_norm[0].reshape(1, vw).astype(F32)
    mix, mo_meta, c_p, n_p, m_p = _mlstm_prompt(
        z, chunk_rows(log_i[:seq]), chunk_rows(log_f[:seq]),
        lane_rows(log_i[seq:seq + N_META].T), lane_rows(log_f[seq:seq + N_META].T), norm_m, seq, heads_m, dk, dv,
        vw + dw)

    t_pad = 8
    zs = jnp.pad(z[seq + N_META:].reshape(dec_b, dec_t, c_gate), ((0, 0), (0, t_pad - dec_t), (0, 0)))
    gate_s = lambda a: lane_rows(a[seq + N_META:].reshape(dec_b, dec_t, heads_m).transpose(0, 2, 1))
    m0 = jnp.broadcast_to(state_m[0][:, :, None], (dec_b, heads_m, LANES))
    mo_s, c_s, n_s_new, m_s = _mlstm_sample(zs, gate_s(log_i), gate_s(log_f), state_C[0], state_n[0], m0, norm_m,
                                             heads_m, dk, dv, dec_t)

    pos = np.concatenate([N_META + np.arange(seq), np.arange(N_META), past + np.tile(np.arange(dec_t), dec_b)])
    cos, sin = _rope_tables(pos)
    q0, q1, k_rot, kb, vb = _rope(zd, 0, 1, 2, dw, cos, sin)
    sub = (diff_subln[0].astype(F32) * (1.0 - lam_init)).reshape(1, DIFF_DV)
    lam_p = diff_lambda_qk[0].astype(F32)
    mix = _flash(q0, q1, kb, vb, lam_p, sub, mix, seq, lam_init)

    pad128 = lambda a: jnp.pad(a, ((0, LANES - a.shape[0]), (0, 0)))
    meta = slice(seq, seq + N_META)
    do_meta = _meta_attn(pad128(q0[meta]), pad128(q1[meta]), pad128(kb[meta]), pad128(vb[meta]), lam_p, sub,
                         lam_init)[:N_META]

    qs = (q0[seq + N_META:] + q1[seq + N_META:]).reshape(dec_b, 1, dec_t, dw)
    rows_q = heads_d * 2 * dec_t
    qbd = jnp.broadcast_to(qs, (dec_b, heads_d * 2, dec_t, dw)).reshape(dec_b, rows_q, dw)
    r_id = np.arange(rows_q)[:, None] // dec_t
    c_id = np.arange(dw)[None, :] // DIFF_DH
    qbd = jnp.where(jnp.asarray(r_id == c_id), qbd, jnp.zeros_like(qbd))
    new_rows = lambda a: jnp.pad(a[seq + N_META:].reshape(dec_b, dec_t, dw), ((0, 0), (0, LANES - dec_t), (0, 0)))
    n_phys, page = cache_k.shape[1], cache_k.shape[2]
    cache_kt = jnp.transpose(cache_k[0], (0, 2, 3, 4, 1)).reshape(n_phys, dw, page)
    do_s = _sample_attn(page_table, qbd, cache_kt, cache_v[0].reshape(n_phys, page * heads_d, DIFF_DV),
                        new_rows(kb).transpose(0, 2, 1), new_rows(vb), lam_p, sub, dec_t, lam_init)
    do_s = do_s.reshape(dec_b, heads_d, 2, dec_t, DIFF_DV)[:, :, 0].transpose(0, 2, 1, 3).reshape(n_s, dw)

    tail = jnp.concatenate([
        jnp.concatenate([mo_meta, mo_s[:, :dec_t].reshape(n_s, vw).astype(BF16)], axis=0),
        jnp.concatenate([do_meta.astype(BF16), do_s.astype(BF16)], axis=0)], axis=1)
    mix = lax.dynamic_update_slice(mix, tail, (seq, 0))
    h = _mm_resid(mix, w_out[0], h, 1.0, _row_tile(rows, 1088), 512)

    h = _ffn(h, norm_ffn2[0], *ffn_weights(ffn2_w_gate[0], ffn2_w_up[0], ffn2_w_down[0]))

    y_prompt = _rmsnorm(h, norm_final, F32, 256, 0, seq // 256).reshape(1, seq, d_model)
    y_sample = _rmsnorm(h, norm_final, F32, N_META, (seq + N_META) // N_META, n_s // N_META)
    y_sample = y_sample.reshape(dec_b, dec_t, d_model)

    order = lambda a: jnp.concatenate([a[meta], a[:seq]], axis=0)
    v_all = zd[:, 2 * dw:3 * dw]
    k_prompt = order(k_rot).reshape(1, 1, seq + N_META, heads_d, 2, DIFF_DH)
    v_prompt = order(v_all).reshape(1, 1, seq + N_META, heads_d, DIFF_DV)
    k_sample = k_rot[seq + N_META:].reshape(1, dec_b, dec_t, heads_d, 2, DIFF_DH)
    v_sample = v_all[seq + N_META:].reshape(1, dec_b, dec_t, heads_d, DIFF_DV)
    return (y_prompt, y_sample, k_prompt, v_prompt,
            c_p[None, None], n_p[None, None], m_p[None, None, :, 0],
            k_sample, v_sample, c_s[None], n_s_new[None], m_s[None, :, :, 0])
```

```python
import functools
import math

import numpy as np
import jax
import jax.numpy as jnp
from jax import lax
from jax.experimental import pallas as pl
from jax.experimental.pallas import tpu as pltpu

F32 = jnp.float32
BF16 = jnp.bfloat16

N_META = 16
CHUNK = 64
ROPE_THETA = 10000.0
GATE_CAP = 15.0
M_INIT = -1e30
EPS = 1e-6
DIFF_DV = 128
DIFF_DH = DIFF_DV // 2
DIFF_SCALE = DIFF_DH ** -0.5
Q_SCALE_LOG2 = DIFF_SCALE * math.log2(math.e)
PAGES_PER_STEP = 4
FLASH_Q_TILE = 2048
FLASH_K_TILE = 512
MASK_VALUE = -1e30

LANES = 128
V7X_VMEM_BYTES = 64 * 2 ** 20
VMEM_LIMIT = V7X_VMEM_BYTES - 8 * 2 ** 20


def _params(*sem):
    return pltpu.CompilerParams(dimension_semantics=sem, vmem_limit_bytes=VMEM_LIMIT)


def _round_up(x, m):
    return (x + m - 1) // m * m


def _row_tile(rows, target):
    n = pl.cdiv(rows, target)
    return _round_up(pl.cdiv(rows, n), 16)


def _col_tile(n):
    return 512 if n % 512 == 0 else 256


def _dot(a, b):
    return jnp.dot(a.astype(BF16), b.astype(BF16), preferred_element_type=F32)


def _dot_nt(a, b):
    return lax.dot_general(a.astype(BF16), b.astype(BF16), (((1,), (1,)), ((), ())),
                           preferred_element_type=F32)


def _dot_tn(a, b):
    return lax.dot_general(a.astype(BF16), b.astype(BF16), (((0,), (0,)), ((), ())),
                           preferred_element_type=F32)


def _rmsnorm_kernel(x_ref, g_ref, o_ref):
    x = x_ref[...]
    y = x * lax.rsqrt(jnp.mean(x * x, axis=-1, keepdims=True) + EPS)
    o_ref[...] = (y * g_ref[...]).astype(o_ref.dtype)


def _rmsnorm(x, g, out_dtype, row_block, first_block=0, n_blocks=None):
    rows, d = x.shape
    if n_blocks is None:
        n_blocks, out_rows = pl.cdiv(rows, row_block), rows
    else:
        out_rows = n_blocks * row_block
    return pl.pallas_call(
        _rmsnorm_kernel,
        grid=(n_blocks,),
        in_specs=[pl.BlockSpec((row_block, d), lambda i: (i + first_block, 0)),
                  pl.BlockSpec((1, d), lambda i: (0, 0))],
        out_specs=pl.BlockSpec((row_block, d), lambda i: (i, 0)),
        out_shape=jax.ShapeDtypeStruct((out_rows, d), out_dtype),
        compiler_params=_params("parallel"),
        name="rmsnorm",
    )(x, g.reshape(1, d).astype(F32))


def _gateup_kernel(x_ref, wg_ref, wu_ref, o_ref):
    x = x_ref[...]
    g = jnp.dot(x, wg_ref[...].astype(BF16), preferred_element_type=F32)
    u = jnp.dot(x, wu_ref[...].astype(BF16), preferred_element_type=F32)
    o_ref[...] = (g * jax.nn.sigmoid(g) * u).astype(o_ref.dtype)


def _gateup(x, wg, wu, tm, tn):
    rows, d = x.shape
    n = wg.shape[1]
    return pl.pallas_call(
        _gateup_kernel,
        grid=(pl.cdiv(rows, tm), n // tn),
        in_specs=[pl.BlockSpec((tm, d), lambda i, j: (i, 0)),
                  pl.BlockSpec((d, tn), lambda i, j: (0, j)),
                  pl.BlockSpec((d, tn), lambda i, j: (0, j))],
        out_specs=pl.BlockSpec((tm, tn), lambda i, j: (i, j)),
        out_shape=jax.ShapeDtypeStruct((rows, n), BF16),
        compiler_params=_params("parallel", "arbitrary"),
        name="ffn_gate_up",
    )(x, wg, wu)


def _mm_resid_kernel(a_ref, w_ref, r_ref, o_ref, *, scale):
    acc = jnp.dot(a_ref[...], w_ref[...].astype(BF16), preferred_element_type=F32)
    o_ref[...] = r_ref[...] + scale * acc


def _mm_resid(a, w, resid, scale, tm, tn):
    rows, k = a.shape
    n = w.shape[1]
    return pl.pallas_call(
        functools.partial(_mm_resid_kernel, scale=scale),
        grid=(pl.cdiv(rows, tm), n // tn),
        in_specs=[pl.BlockSpec((tm, k), lambda i, j: (i, 0)),
                  pl.BlockSpec((k, tn), lambda i, j: (0, j)),
                  pl.BlockSpec((tm, tn), lambda i, j: (i, j))],
        out_specs=pl.BlockSpec((tm, tn), lambda i, j: (i, j)),
        out_shape=jax.ShapeDtypeStruct((rows, n), F32),
        compiler_params=_params("parallel", "arbitrary"),
        name="matmul_residual",
    )(a, w, resid)


def _mm_nt_kernel(a_ref, wt_ref, o_ref):
    o_ref[...] = _dot_nt(a_ref[...], wt_ref[...])


def _mm_nt(a, wt, tm, tn, n=None):
    rows, k = a.shape
    n = wt.shape[0] if n is None else n
    return pl.pallas_call(
        _mm_nt_kernel,
        grid=(pl.cdiv(rows, tm), n // tn),
        in_specs=[pl.BlockSpec((tm, k), lambda i, j: (i, 0)),
                  pl.BlockSpec((tn, k), lambda i, j: (j, 0))],
        out_specs=pl.BlockSpec((tm, tn), lambda i, j: (i, j)),
        out_shape=jax.ShapeDtypeStruct((rows, n), F32),
        compiler_params=_params("parallel", "arbitrary"),
        name="matmul_nt",
    )(a, wt)


def _ffn(h, g, wg, wu, wd):
    rows = h.shape[0]
    xn = _rmsnorm(h, g, BF16, 256)
    hid = _gateup(xn, wg, wu, _row_tile(rows, 1088), _col_tile(wg.shape[1]))
    return _mm_resid(hid, wd, h, 0.5, _row_tile(rows, 544), 512)


def _gate_kernel(z_ref, b_ref, o_ref, *, heads):
    z = z_ref[...] + b_ref[...]
    c = GATE_CAP * jnp.tanh(z / GATE_CAP)
    log_sig = jnp.minimum(c, 0.0) - jnp.log1p(jnp.exp(-jnp.abs(c)))
    lane = lax.broadcasted_iota(jnp.int32, c.shape, 1)
    o_ref[...] = jnp.where(lane < heads, c, log_sig)


def _gates(z, bias, heads):
    rows, d = z.shape
    rb = 512
    return pl.pallas_call(
        functools.partial(_gate_kernel, heads=heads),
        grid=(pl.cdiv(rows, rb),),
        in_specs=[pl.BlockSpec((rb, d), lambda i: (i, 0)), pl.BlockSpec((1, d), lambda i: (0, 0))],
        out_specs=pl.BlockSpec((rb, d), lambda i: (i, 0)),
        out_shape=jax.ShapeDtypeStruct((rows, d), F32),
        compiler_params=_params("parallel"),
        name="mlstm_gates",
    )(z, bias)


def _proj_rope_kernel(a_ref, wt_ref, cos_ref, sin_ref, o0_ref, o1_ref, *, mode):
    z = _dot_nt(a_ref[...], wt_ref[...])
    if mode == "v":
        o0_ref[...] = z
        o1_ref[...] = z.astype(BF16)
        return
    cos = cos_ref[...]
    sin = sin_ref[...]
    lane = lax.broadcasted_iota(jnp.int32, cos.shape, 1)
    first_half = (lane % DIFF_DH) < (DIFF_DH // 2)
    map0 = lane < DIFF_DH
    for j in range(z.shape[1] // LANES):
        sl = slice(j * LANES, (j + 1) * LANES)
        x = z[:, sl]
        partner = jnp.where(first_half, pltpu.roll(x, LANES - DIFF_DH // 2, 1), pltpu.roll(x, DIFF_DH // 2, 1))
        r = x * cos + partner * sin
        if mode == "q":
            r = r * Q_SCALE_LOG2
            o0_ref[:, sl] = jnp.where(map0, r, 0.0).astype(BF16)
            o1_ref[:, sl] = jnp.where(map0, 0.0, r).astype(BF16)
        else:
            o0_ref[:, sl] = r
            o1_ref[:, sl] = r.astype(BF16)


def _proj_rope(a, wt, row0, width, cos, sin, mode, tm, tn):
    rows, k = a.shape
    off = row0 // tn
    tspec = pl.BlockSpec((tm, LANES), lambda i, j: (i, 0))
    ospec = pl.BlockSpec((tm, tn), lambda i, j: (i, j))
    return pl.pallas_call(
        functools.partial(_proj_rope_kernel, mode=mode),
        grid=(pl.cdiv(rows, tm), width // tn),
        in_specs=[pl.BlockSpec((tm, k), lambda i, j: (i, 0)),
                  pl.BlockSpec((tn, k), lambda i, j: (j + off, 0)), tspec, tspec],
        out_specs=[ospec, ospec],
        out_shape=[jax.ShapeDtypeStruct((rows, width), BF16 if mode == "q" else F32),
                   jax.ShapeDtypeStruct((rows, width), BF16)],
        compiler_params=_params("parallel", "arbitrary"),
        name="proj_rope_" + mode,
    )(a, wt, cos, sin)


def _lane_cumsum(x):
    lane = lax.broadcasted_iota(jnp.int32, x.shape, 1)
    k = 1
    while k < x.shape[1]:
        x = x + jnp.where(lane >= k, pltpu.roll(x, k, 1), 0.0)
        k *= 2
    return x


def _mlstm_heads(load, store, c_ref, n_ref, m_ref, heads, valid):
    ins = [load(h) for h in range(heads)]
    t, dk = ins[0][0].shape
    ri = lax.broadcasted_iota(jnp.int32, (t, t), 0)
    ci = lax.broadcasted_iota(jnp.int32, (t, t), 1)
    eye = ri == ci
    lane = lax.broadcasted_iota(jnp.int32, (1, t), 1)

    stage1 = []
    for h, (q, k, v, o, li_row, b_row, norm_row) in enumerate(ins):
        k = k * (dk ** -0.5)
        c_state = c_ref[h]
        stage1.append((k, c_state, _dot_nt(q, k), _dot(q, c_state)))

    stage2 = []
    for h, (q, k_raw, v, o, li_row, b_row, norm_row) in enumerate(ins):
        k, c_state, qk, qc = stage1[h]
        m_prev = m_ref[h:h + 1, 0:1]
        b_mat = jnp.broadcast_to(b_row, (t, t))
        b_col = jnp.sum(jnp.where(eye, b_mat, 0.0), axis=1, keepdims=True)
        log_d = jnp.where(ci <= ri, b_col - b_mat + li_row, -jnp.inf)
        m_inter = m_prev + b_col
        m_t = jnp.maximum(m_inter, jnp.max(log_d, axis=1, keepdims=True))
        w_inter = jnp.exp(m_inter - m_t)
        s = qk * jnp.exp(log_d - m_t)
        m_new = m_t[valid - 1:valid, :]
        decay = jnp.exp(m_inter[valid - 1:valid, :] - m_new)
        wk_row = jnp.where(lane < valid, jnp.exp(b_row[:, valid - 1:valid] - b_row + li_row - m_new), 0.0)
        wk_col = jnp.sum(jnp.where(eye, jnp.broadcast_to(wk_row, (t, t)), 0.0), axis=1, keepdims=True)
        stage2.append((s, k * wk_col, w_inter, m_t, m_new, decay))

    stage3 = []
    for h, (q, k_raw, v, o, li_row, b_row, norm_row) in enumerate(ins):
        s, kw = stage2[h][:2]
        stage3.append((_dot(s, v), _dot_tn(kw, v)))

    for h, (q, k_raw, v, o, li_row, b_row, norm_row) in enumerate(ins):
        k, c_state, qk, qc = stage1[h]
        s, kw, w_inter, m_t, m_new, decay = stage2[h]
        sv, kv = stage3[h]
        n_row = n_ref[h:h + 1, :]
        num = w_inter * qc + sv
        den = w_inter * jnp.sum(q * n_row, axis=1, keepdims=True) + jnp.sum(s, axis=1, keepdims=True)
        y = num / jnp.maximum(jnp.abs(den), jnp.exp(-m_t))
        c_ref[h] = decay * c_state + kv
        n_ref[h:h + 1, :] = decay * n_row + jnp.sum(kw, axis=0, keepdims=True)
        m_ref[h:h + 1, :] = jnp.broadcast_to(m_new, (1, m_ref.shape[1]))
        yn = y * lax.rsqrt(jnp.mean(y * y, axis=-1, keepdims=True) + EPS) * norm_row
        store(h, jax.nn.sigmoid(o) * yn)


def _mlstm_prompt_kernel(q_ref, k_ref, v_ref, o_ref, li_ref, lf_ref,
                         qm_ref, km_ref, vm_ref, om_ref, lim_ref, lfm_ref, norm_ref,
                         out_ref, outm_ref, c_ref, n_ref, m_ref, *, heads, dk, dv, chunks):
    g = pl.program_id(0)

    @pl.when(g == 0)
    def _():
        c_ref[...] = jnp.zeros_like(c_ref)
        n_ref[...] = jnp.zeros_like(n_ref)
        m_ref[...] = jnp.full_like(m_ref, M_INIT)
        bm = _lane_cumsum(lfm_ref[...])

        def load_meta(h):
            return (qm_ref[:, h * dk:(h + 1) * dk], km_ref[:, h * dk:(h + 1) * dk],
                    vm_ref[:, h * dv:(h + 1) * dv], om_ref[:, h * dv:(h + 1) * dv],
                    lim_ref[h:h + 1, 0:N_META], bm[h:h + 1, 0:N_META], norm_ref[:, h * dv:(h + 1) * dv])

        def store_meta(h, y):
            outm_ref[:, h * dv:(h + 1) * dv] = y.astype(outm_ref.dtype)

        _mlstm_heads(load_meta, store_meta, c_ref, n_ref, m_ref, heads, N_META)

    def body(c, carry):
        r0 = pl.multiple_of(c * CHUNK, CHUNK)
        rows = pl.ds(r0, CHUNK)
        gc = g * chunks + c

        def load(h):
            return (q_ref[rows, h * dk:(h + 1) * dk], k_ref[rows, h * dk:(h + 1) * dk],
                    v_ref[rows, h * dv:(h + 1) * dv], o_ref[rows, h * dv:(h + 1) * dv],
                    li_ref[h, pl.ds(gc, 1), :][:, 0:CHUNK], _lane_cumsum(lf_ref[h, pl.ds(gc, 1), :])[:, 0:CHUNK],
                    norm_ref[:, h * dv:(h + 1) * dv])

        def store(h, y):
            out_ref[rows, h * dv:(h + 1) * dv] = y.astype(out_ref.dtype)

        _mlstm_heads(load, store, c_ref, n_ref, m_ref, heads, CHUNK)
        return carry

    lax.fori_loop(0, chunks, body, 0)


def _mlstm_prompt(z, li, lf, li_meta, lf_meta, norm, seq, heads, dk, dv, mix_width):
    chunks = 4
    rb = chunks * CHUNK
    qw, vw = heads * dk, heads * dv
    meta_blk = seq // N_META
    spec = lambda w, c: pl.BlockSpec((rb, w), lambda g, c=c: (g, c))
    mspec = lambda w, c: pl.BlockSpec((N_META, w), lambda g, c=c: (meta_blk, c))
    gspec = pl.BlockSpec((heads, seq // CHUNK, LANES), lambda g: (0, 0, 0))
    const2 = lambda shape: pl.BlockSpec(shape, lambda g: (0,) * len(shape))
    return pl.pallas_call(
        functools.partial(_mlstm_prompt_kernel, heads=heads, dk=dk, dv=dv, chunks=chunks),
        grid=(seq // rb,),
        in_specs=[spec(qw, 0), spec(qw, 1), spec(vw, 1), spec(vw, 2), gspec, gspec,
                  mspec(qw, 0), mspec(qw, 1), mspec(vw, 1), mspec(vw, 2),
                  const2((heads, LANES)), const2((heads, LANES)), const2((1, vw))],
        out_specs=[pl.BlockSpec((rb, vw), lambda g: (g, 0)), const2((N_META, vw)),
                   const2((heads, dk, dv)), const2((heads, dk)), const2((heads, LANES))],
        out_shape=[jax.ShapeDtypeStruct((z.shape[0], mix_width), BF16), jax.ShapeDtypeStruct((N_META, vw), BF16),
                   jax.ShapeDtypeStruct((heads, dk, dv), F32), jax.ShapeDtypeStruct((heads, dk), F32),
                   jax.ShapeDtypeStruct((heads, LANES), F32)],
        compiler_params=_params("arbitrary"),
        name="mlstm_prompt",
    )(z, z, z, z, li, lf, z, z, z, z, li_meta, lf_meta, norm)


def _mlstm_sample_kernel(q_ref, k_ref, v_ref, o_ref, li_ref, lf_ref, c0_ref, n0_ref, m0_ref, norm_ref,
                         out_ref, c_ref, n_ref, m_ref, *, heads, dk, dv, valid):
    c_ref[...] = c0_ref[...]
    n_ref[...] = n0_ref[...]
    m_ref[...] = m0_ref[...]
    t = q_ref.shape[0]
    b_all = _lane_cumsum(lf_ref[...])

    def load(h):
        return (q_ref[:, h * dk:(h + 1) * dk], k_ref[:, h * dk:(h + 1) * dk],
                v_ref[:, h * dv:(h + 1) * dv], o_ref[:, h * dv:(h + 1) * dv],
                li_ref[h:h + 1, 0:t], b_all[h:h + 1, 0:t], norm_ref[:, h * dv:(h + 1) * dv])

    def store(h, y):
        out_ref[:, h * dv:(h + 1) * dv] = y

    _mlstm_heads(load, store, c_ref, n_ref, m_ref, heads, valid)


def _mlstm_sample(zs, li, lf, c0, n0, m0, norm, heads, dk, dv, valid):
    nb, t, _ = zs.shape
    qw, vw = heads * dk, heads * dv
    spec = lambda w, c: pl.BlockSpec((None, t, w), lambda b, c=c: (b, 0, c))
    per_b = lambda *shape: pl.BlockSpec((None,) + shape, lambda b: (b,) + (0,) * len(shape))
    return pl.pallas_call(
        functools.partial(_mlstm_sample_kernel, heads=heads, dk=dk, dv=dv, valid=valid),
        grid=(nb,),
        in_specs=[spec(qw, 0), spec(qw, 1), spec(vw, 1), spec(vw, 2), per_b(heads, LANES), per_b(heads, LANES),
                  per_b(heads, dk, dv), per_b(heads, dk), per_b(heads, LANES),
                  pl.BlockSpec((1, vw), lambda b: (0, 0))],
        out_specs=[per_b(t, vw), per_b(heads, dk, dv), per_b(heads, dk), per_b(heads, LANES)],
        out_shape=[jax.ShapeDtypeStruct((nb, t, vw), F32), jax.ShapeDtypeStruct((nb, heads, dk, dv), F32),
                   jax.ShapeDtypeStruct((nb, heads, dk), F32), jax.ShapeDtypeStruct((nb, heads, LANES), F32)],
        compiler_params=_params("parallel"),
        name="mlstm_sample",
    )(zs, zs, zs, zs, li, lf, c0, n0, m0, norm)


def _lambda_value(lp_ref, lam_init):
    lp = lp_ref[...]
    a = jnp.sum(lp[0:1, :] * lp[1:2, :], axis=1, keepdims=True)
    b = jnp.sum(lp[2:3, :] * lp[3:4, :], axis=1, keepdims=True)
    return jnp.exp(a) - jnp.exp(b) + lam_init


def _scores_t(q, kb, mask):
    s = _dot_nt(kb, q)
    return s if mask is None else jnp.where(mask, s, MASK_VALUE)


def _softmax_step_t(q, kb, vb, m_ref, l_ref, acc_ref, mask):
    _softmax_update_t(_scores_t(q, kb, mask), vb, m_ref, l_ref, acc_ref)


def _softmax_update_t(s, vb, m_ref, l_ref, acc_ref, cols=slice(None)):
    m_old = m_ref[:, cols]
    m_new = jnp.maximum(m_old, jnp.max(s, axis=0, keepdims=True))
    alpha = jnp.exp2(m_old - m_new)
    p = jnp.exp2(s - m_new)
    l_ref[:, cols] = alpha * l_ref[:, cols] + jnp.sum(p, axis=0, keepdims=True)
    acc_ref[:, cols] = alpha * acc_ref[:, cols] + _dot_tn(vb, p)
    m_ref[:, cols] = m_new


def _diff_finish_t(lam, sub_ref, l0_ref, a0_ref, l1_ref, a1_ref):
    o_t = a0_ref[...] / l0_ref[...] - lam * (a1_ref[...] / l1_ref[...])
    o = o_t.T
    return o * lax.rsqrt(jnp.mean(o * o, axis=-1, keepdims=True) + EPS) * sub_ref[...]


def _init_softmax_state(m_ref, l_ref, acc_ref):
    m_ref[...] = jnp.full_like(m_ref, MASK_VALUE)
    l_ref[...] = jnp.zeros_like(l_ref)
    acc_ref[...] = jnp.zeros_like(acc_ref)


def _flash_kernel(q0_ref, q1_ref, k_ref, v_ref, lp_ref, sub_ref, mix_ref, out_ref, *state, seq, tq, tk, lam_init):
    del mix_ref
    qi = pl.program_id(1)
    groups = tq // tk
    chains = [state[6 * g:6 * g + 6] for g in range(groups)]
    km, vm = k_ref[seq:seq + N_META, :], v_ref[seq:seq + N_META, :]
    queries = []
    for g, (m0, l0, a0, m1, l1, a1) in enumerate(chains):
        q0 = q0_ref[g * tk:(g + 1) * tk, :]
        q1 = q1_ref[g * tk:(g + 1) * tk, :]
        queries.append((q0, q1))
        _init_softmax_state(m0, l0, a0)
        _init_softmax_state(m1, l1, a1)
        _softmax_step_t(q0, km, vm, m0, l0, a0, None)
        _softmax_step_t(q1, km, vm, m1, l1, a1, None)

    ki = lax.broadcasted_iota(jnp.int32, (tk, tk), 0)
    qj = lax.broadcasted_iota(jnp.int32, (tk, tk), 1)
    causal = ki <= qj

    def step(j, first_group, diagonal):
        rows = pl.ds(pl.multiple_of(j * tk, tk), tk)
        kb, vb = k_ref[rows, :], v_ref[rows, :]
        scores = []
        for g in range(first_group, groups):
            mask = causal if (diagonal and g == first_group) else None
            scores.append((_scores_t(queries[g][0], kb, mask), _scores_t(queries[g][1], kb, mask)))
        for g, (s0, s1) in zip(range(first_group, groups), scores):
            m0, l0, a0, m1, l1, a1 = chains[g]
            _softmax_update_t(s0, vb, m0, l0, a0)
            _softmax_update_t(s1, vb, m1, l1, a1)

    def body(j, carry):
        step(j, 0, False)
        return carry

    lax.fori_loop(0, qi * groups, body, 0)
    for g in range(groups):
        step(qi * groups + g, g, True)

    lam = _lambda_value(lp_ref, lam_init)
    for g, (m0, l0, a0, m1, l1, a1) in enumerate(chains):
        out_ref[g * tk:(g + 1) * tk, :] = _diff_finish_t(lam, sub_ref, l0, a0, l1, a1).astype(out_ref.dtype)


def _flash(q0, q1, kb, vb, lam_p, subln, mix, seq, lam_init):
    width = q0.shape[1]
    heads = width // DIFF_DV
    col0 = (mix.shape[1] - width) // DIFF_DV
    tq = min(FLASH_Q_TILE, seq)
    tk = min(FLASH_K_TILE, tq)
    kv_rows = seq + N_META
    qspec = pl.BlockSpec((tq, DIFF_DV), lambda h, i: (i, h))
    kspec = pl.BlockSpec((kv_rows, DIFF_DV), lambda h, i: (0, h))
    stat = pltpu.VMEM((1, tk), F32)
    acc = pltpu.VMEM((DIFF_DV, tk), F32)
    return pl.pallas_call(
        functools.partial(_flash_kernel, seq=seq, tq=tq, tk=tk, lam_init=lam_init),
        grid=(heads, seq // tq),
        in_specs=[qspec, qspec, kspec, kspec,
                  pl.BlockSpec(lam_p.shape, lambda h, i: (0, 0)), pl.BlockSpec((1, DIFF_DV), lambda h, i: (0, 0)),
                  pl.BlockSpec(memory_space=pl.ANY)],
        out_specs=pl.BlockSpec((tq, DIFF_DV), lambda h, i: (i, col0 + h)),
        out_shape=jax.ShapeDtypeStruct(mix.shape, mix.dtype),
        input_output_aliases={6: 0},
        scratch_shapes=[stat, stat, acc, stat, stat, acc] * (tq // tk),
        compiler_params=_params("parallel", "arbitrary"),
        name="diff_attn_prompt",
    )(q0, q1, kb, vb, lam_p, subln, mix)


def _meta_attn_kernel(q0_ref, q1_ref, k_ref, v_ref, lp_ref, sub_ref, out_ref,
                      m0, l0, a0, m1, l1, a1, *, lam_init):
    _init_softmax_state(m0, l0, a0)
    _init_softmax_state(m1, l1, a1)
    t = q0_ref.shape[0]
    ki = lax.broadcasted_iota(jnp.int32, (t, t), 0)
    qj = lax.broadcasted_iota(jnp.int32, (t, t), 1)
    mask = ki <= qj
    _softmax_step_t(q0_ref[...], k_ref[...], v_ref[...], m0, l0, a0, mask)
    _softmax_step_t(q1_ref[...], k_ref[...], v_ref[...], m1, l1, a1, mask)
    out_ref[...] = _diff_finish_t(_lambda_value(lp_ref, lam_init), sub_ref, l0, a0, l1, a1)


def _meta_attn(q0, q1, kb, vb, lam_p, subln, lam_init):
    t, width = q0.shape
    heads = width // DIFF_DV
    spec = pl.BlockSpec((t, DIFF_DV), lambda h: (0, h))
    stat = pltpu.VMEM((1, t), F32)
    acc = pltpu.VMEM((DIFF_DV, t), F32)
    return pl.pallas_call(
        functools.partial(_meta_attn_kernel, lam_init=lam_init),
        grid=(heads,),
        in_specs=[spec, spec, spec, spec,
                  pl.BlockSpec(lam_p.shape, lambda h: (0, 0)), pl.BlockSpec((1, DIFF_DV), lambda h: (0, 0))],
        out_specs=spec,
        out_shape=jax.ShapeDtypeStruct((t, width), F32),
        scratch_shapes=[stat, stat, acc, stat, stat, acc],
        compiler_params=_params("parallel"),
        name="diff_attn_meta",
    )(q0, q1, kb, vb, lam_p, subln)


def _sample_attn_kernel(pt_ref, q_ref, *refs, pages, heads, new_valid, q_len, lam_init):
    k_refs = refs[:pages]
    v_refs = refs[pages:2 * pages]
    kn_ref, vn_ref, lp_ref, sub_ref, out_ref, kcat, vcat, m_s, l_s, acc_s = refs[2 * pages:]
    j = pl.program_id(1)
    page = k_refs[0].shape[1]
    keys = pages * page
    rows_q = q_ref.shape[0]
    group = rows_q // heads

    @pl.when(j == 0)
    def _():
        m_s[...] = jnp.full_like(m_s, MASK_VALUE)
        l_s[...] = jnp.zeros_like(l_s)
        acc_s[...] = jnp.zeros_like(acc_s)

    q = q_ref[...]

    def update(k_t, weighted_values, mask):
        s = jnp.dot(q, k_t, preferred_element_type=F32)
        if mask is not None:
            s = jnp.where(mask, s, MASK_VALUE)
        m_old = m_s[...]
        m_new = jnp.maximum(m_old, jnp.max(s, axis=1, keepdims=True))
        alpha = jnp.exp2(m_old - m_new)
        p = jnp.exp2(s - m_new)
        l_s[...] = alpha * l_s[...] + jnp.sum(p, axis=1, keepdims=True)
        acc_s[...] = alpha * acc_s[...] + weighted_values(p)
        m_s[...] = m_new

    def cache_values(p):
        own_head = (lax.broadcasted_iota(jnp.int32, (heads, rows_q), 0)
                    == lax.broadcasted_iota(jnp.int32, (heads, rows_q), 1) // group)
        p_t = p.T
        expanded = jnp.where(own_head[None], p_t[:, None, :], 0.0).reshape(keys * heads, rows_q)
        return _dot_tn(expanded, vcat[...])

    def new_values(p):
        return jnp.concatenate(
            [jnp.dot(p[h * group:(h + 1) * group, :].astype(BF16), vn_ref[:, h * DIFF_DV:(h + 1) * DIFF_DV],
                     preferred_element_type=F32) for h in range(heads)], axis=0)

    for i in range(pages):
        kcat[:, i * page:(i + 1) * page] = k_refs[i][...].astype(BF16)
        vcat[i * page * heads:(i + 1) * page * heads, :] = v_refs[i][...].astype(BF16)
    update(kcat[...], cache_values, None)

    @pl.when(j == pl.num_programs(1) - 1)
    def _():
        n_new = kn_ref.shape[1]
        r = lax.broadcasted_iota(jnp.int32, (rows_q, n_new), 0)
        u = lax.broadcasted_iota(jnp.int32, (rows_q, n_new), 1)
        update(kn_ref[...], new_values, (u <= r % q_len) & (u < new_valid))
        on = acc_s[...] / l_s[...]
        lam = _lambda_value(lp_ref, lam_init)
        d = on - lam * pltpu.roll(on, rows_q - q_len, 0)
        out_ref[...] = d * lax.rsqrt(jnp.mean(d * d, axis=-1, keepdims=True) + EPS) * sub_ref[...]


def _sample_attn(page_table, qbd, cache_kt, cache_v, k_new_t, v_new, lam_p, subln, q_len, lam_init):
    nb, rows_q, width = qbd.shape
    heads = width // DIFF_DV
    page = cache_kt.shape[2]
    n_pages = page_table.shape[1]
    pages = PAGES_PER_STEP
    k_spec = lambda i: pl.BlockSpec((None, width, page), lambda b, j, pt, i=i: (pt[b, j * pages + i], 0, 0))
    v_spec = lambda i: pl.BlockSpec((None, page * heads, DIFF_DV),
                                    lambda b, j, pt, i=i: (pt[b, j * pages + i], 0, 0))
    per_b = lambda shape: pl.BlockSpec((None,) + shape, lambda b, j, pt: (b, 0, 0))
    grid_spec = pltpu.PrefetchScalarGridSpec(
        num_scalar_prefetch=1,
        grid=(nb, n_pages // pages),
        in_specs=[per_b((rows_q, width))] + [k_spec(i) for i in range(pages)] + [v_spec(i) for i in range(pages)]
                 + [per_b(k_new_t.shape[1:]), per_b(v_new.shape[1:]),
                    pl.BlockSpec(lam_p.shape, lambda b, j, pt: (0, 0)),
                    pl.BlockSpec((1, DIFF_DV), lambda b, j, pt: (0, 0))],
        out_specs=pl.BlockSpec((None, rows_q, DIFF_DV), lambda b, j, pt: (b, 0, 0)),
        scratch_shapes=[pltpu.VMEM((width, pages * page), BF16), pltpu.VMEM((heads * pages * page, DIFF_DV), BF16),
                        pltpu.VMEM((rows_q, 1), F32), pltpu.VMEM((rows_q, 1), F32),
                        pltpu.VMEM((rows_q, DIFF_DV), F32)],
    )
    return pl.pallas_call(
        functools.partial(_sample_attn_kernel, pages=pages, heads=heads, new_valid=q_len, q_len=q_len,
                          lam_init=lam_init),
        grid_spec=grid_spec,
        out_shape=jax.ShapeDtypeStruct((nb, rows_q, DIFF_DV), F32),
        compiler_params=_params("parallel", "arbitrary"),
        name="diff_attn_sample",
    )(page_table, qbd, *([cache_kt] * pages), *([cache_v] * pages), k_new_t, v_new, lam_p, subln)


def _rope_tables(pos):
    half = DIFF_DH // 2
    inv = ROPE_THETA ** (-jnp.arange(half, dtype=F32) / half)
    ang = jnp.asarray(pos, F32)[:, None] * inv[None, :]
    cos, sin = jnp.cos(ang), jnp.sin(ang)
    cos = jnp.tile(cos, (1, LANES // half))
    sin = jnp.tile(jnp.concatenate([-sin, sin], axis=1), (1, LANES // DIFF_DH))
    return cos, sin


def kernel(x_prompt, x_sample, cache_k, cache_v, state_C, state_n, state_m, page_table, meta_tokens,
           norm_ffn1, ffn1_w_gate, ffn1_w_up, ffn1_w_down, norm_mix, w_in, mlstm_b_i, mlstm_b_f,
           mlstm_norm, diff_lambda_qk, diff_subln, w_out, norm_ffn2, ffn2_w_gate, ffn2_w_up,
           ffn2_w_down, norm_final):
    batch, seq, d_model = x_prompt.shape
    assert batch == 1 and state_C.shape[0] == 1
    dec_b, dec_t, _ = x_sample.shape
    heads_m = mlstm_b_i.shape[1]
    dk, dv = state_C.shape[3], state_C.shape[4]
    qw, vw = heads_m * dk, heads_m * dv
    dw = w_out.shape[1] - vw
    heads_d = dw // DIFF_DV
    assert vw == 2 * qw and dw == vw, "column blocks of the projection are addressed in units of the q width"
    n_s = dec_b * dec_t
    rows = seq + N_META + n_s
    past = page_table.shape[1] * cache_k.shape[2]
    lam_init = 0.8 - 0.6 * math.exp(-0.0)

    def ffn_weights(wg, wu, wd):
        return wg, wu, wd.astype(BF16)

    h = jnp.concatenate([x_prompt[0], meta_tokens.astype(F32), x_sample.reshape(n_s, d_model)], axis=0)

    h = _ffn(h, norm_ffn1[0], *ffn_weights(ffn1_w_gate[0], ffn1_w_up[0], ffn1_w_down[0]))

    wt = w_in[0].T
    c_gate = 2 * qw + 2 * vw
    wt_gate = jnp.pad(wt[c_gate:c_gate + 2 * heads_m], ((0, LANES - 2 * heads_m), (0, 0)))
    u = _rmsnorm(h, norm_mix[0], BF16, 256)
    tm = _row_tile(rows, 1088)
    z = _mm_nt(u, wt, tm, _col_tile(c_gate), c_gate)
    zg = _mm_nt(u, wt_gate, tm, LANES)
    pos = np.concatenate([N_META + np.arange(seq), np.arange(N_META), past + np.tile(np.arange(dec_t), dec_b)])
    cos, sin = _rope_tables(pos)
    wt_d = wt[c_gate + 2 * heads_m:]
    tn_d = _col_tile(dw)
    q0, q1 = _proj_rope(u, wt_d, 0, dw, cos, sin, "q", tm, tn_d)
    k_rot, kb = _proj_rope(u, wt_d, dw, dw, cos, sin, "k", tm, tn_d)
    v_all, vb = _proj_rope(u, wt_d, 2 * dw, dw, cos, sin, "v", tm, tn_d)
    gate_bias = jnp.pad(jnp.concatenate([mlstm_b_i[0], mlstm_b_f[0]]), (0, LANES - 2 * heads_m)).reshape(1, LANES)
    gates = _gates(zg, gate_bias.astype(F32), heads_m)
    log_i, log_f = gates[:, :heads_m], gates[:, heads_m:2 * heads_m]

    def chunk_rows(a):
        a = a.T.reshape(heads_m, seq // CHUNK, CHUNK)
        return jnp.pad(a, ((0, 0), (0, 0), (0, LANES - CHUNK)))

    lane_rows = lambda a: jnp.pad(a, ((0, 0),) * (a.ndim - 1) + ((0, LANES - a.shape[-1]),))
    norm_m = mlstm_norm[0].reshape(1, vw).astype(F32)
    mix, mo_meta, c_p, n_p, m_p = _mlstm_prompt(
        z, chunk_rows(log_i[:seq]), chunk_rows(log_f[:seq]),
        lane_rows(log_i[seq:seq + N_META].T), lane_rows(log_f[seq:seq + N_META].T), norm_m, seq, heads_m, dk, dv,
        vw + dw)

    t_pad = 8
    zs = jnp.pad(z[seq + N_META:].reshape(dec_b, dec_t, c_gate), ((0, 0), (0, t_pad - dec_t), (0, 0)))
    gate_s = lambda a: lane_rows(a[seq + N_META:].reshape(dec_b, dec_t, heads_m).transpose(0, 2, 1))
    m0 = jnp.broadcast_to(state_m[0][:, :, None], (dec_b, heads_m, LANES))
    mo_s, c_s, n_s_new, m_s = _mlstm_sample(zs, gate_s(log_i), gate_s(log_f), state_C[0], state_n[0], m0, norm_m,
                                             heads_m, dk, dv, dec_t)

    sub =(diff_subln[0].astype(F32) * (1.0 - lam_init)).reshape(1, DIFF_DV)
    lam_p = diff_lambda_qk[0].astype(F32)
    mix = _flash(q0, q1, kb, vb, lam_p, sub, mix, seq, lam_init)

    pad128 = lambda a: jnp.pad(a, ((0, LANES - a.shape[0]), (0, 0)))
    meta = slice(seq, seq + N_META)
    do_meta = _meta_attn(pad128(q0[meta]), pad128(q1[meta]), pad128(kb[meta]), pad128(vb[meta]), lam_p, sub,
                         lam_init)[:N_META]

    qs = (q0[seq + N_META:] + q1[seq + N_META:]).reshape(dec_b, 1, dec_t, dw)
    rows_q = heads_d * 2 * dec_t
    qbd = jnp.broadcast_to(qs, (dec_b, heads_d * 2, dec_t, dw)).reshape(dec_b, rows_q, dw)
    r_id = np.arange(rows_q)[:, None] // dec_t
    c_id = np.arange(dw)[None, :] // DIFF_DH
    qbd = jnp.where(jnp.asarray(r_id == c_id), qbd, jnp.zeros_like(qbd))
    new_rows = lambda a: jnp.pad(a[seq + N_META:].reshape(dec_b, dec_t, dw), ((0, 0), (0, LANES - dec_t), (0, 0)))
    n_phys, page = cache_k.shape[1], cache_k.shape[2]
    cache_kt = jnp.transpose(cache_k[0], (0, 2, 3, 4, 1)).reshape(n_phys, dw, page)
    do_s = _sample_attn(page_table, qbd, cache_kt, cache_v[0].reshape(n_phys, page * heads_d, DIFF_DV),
                        new_rows(kb).transpose(0, 2, 1), new_rows(vb), lam_p, sub, dec_t, lam_init)
    do_s = do_s.reshape(dec_b, heads_d, 2, dec_t, DIFF_DV)[:, :, 0].transpose(0, 2, 1, 3).reshape(n_s, dw)

    tail = jnp.concatenate([
        jnp.concatenate([mo_meta, mo_s[:, :dec_t].reshape(n_s, vw).astype(BF16)], axis=0),
        jnp.concatenate([do_meta.astype(BF16), do_s.astype(BF16)], axis=0)], axis=1)
    mix = lax.dynamic_update_slice(mix, tail, (seq, 0))
    h = _mm_resid(mix, w_out[0], h, 1.0, _row_tile(rows, 1088), 512)

    h = _ffn(h, norm_ffn2[0], *ffn_weights(ffn2_w_gate[0], ffn2_w_up[0], ffn2_w_down[0]))

    y_prompt = _rmsnorm(h, norm_final, F32, 256, 0, seq // 256).reshape(1, seq, d_model)
    y_sample = _rmsnorm(h, norm_final, F32, N_META, (seq + N_META) // N_META, n_s // N_META)
    y_sample = y_sample.reshape(dec_b, dec_t, d_model)

    order = lambda a: jnp.concatenate([a[meta], a[:seq]], axis=0)
    k_prompt = order(k_rot).reshape(1, 1, seq + N_META, heads_d, 2, DIFF_DH)
    v_prompt = order(v_all).reshape(1, 1, seq + N_META, heads_d, DIFF_DV)
    k_sample = k_rot[seq + N_META:].reshape(1, dec_b, dec_t, heads_d, 2, DIFF_DH)
    v_sample = v_all[seq + N_META:].reshape(1, dec_b, dec_t, heads_d, DIFF_DV)
    return (y_prompt, y_sample, k_prompt, v_prompt,
            c_p[None, None], n_p[None, None], m_p[None, None, :, 0],
            k_sample, v_sample, c_s[None], n_s_new[None], m_s[None, :, :, 0])
```

```python
import functools
import math

import numpy as np
import jax
import jax.numpy as jnp
from jax import lax
from jax.experimental import pallas as pl
from jax.experimental.pallas import tpu as pltpu

F32 = jnp.float32
BF16 = jnp.bfloat16

N_META = 16
CHUNK = 64
ROPE_THETA = 10000.0
GATE_CAP = 15.0
M_INIT = -1e30
EPS = 1e-6
DIFF_DV = 128
DIFF_DH = DIFF_DV // 2
DIFF_SCALE = DIFF_DH ** -0.5
Q_SCALE_LOG2 = DIFF_SCALE * math.log2(math.e)
PAGES_PER_STEP = 8
PAGES_PER_UPDATE = 4
FLASH_Q_TILE = 2048
FLASH_K_TILE = 512
MASK_VALUE = -1e30

LANES = 128
V7X_VMEM_BYTES = 64 * 2 ** 20
VMEM_LIMIT = V7X_VMEM_BYTES - 8 * 2 ** 20


def _params(*sem):
    return pltpu.CompilerParams(dimension_semantics=sem, vmem_limit_bytes=VMEM_LIMIT)


def _round_up(x, m):
    return (x + m - 1) // m * m


def _row_tile(rows, target):
    n = pl.cdiv(rows, target)
    return _round_up(pl.cdiv(rows, n), 16)


def _col_tile(n):
    return 512 if n % 512 == 0 else 256


def _dot(a, b):
    return jnp.dot(a.astype(BF16), b.astype(BF16), preferred_element_type=F32)


def _dot_nt(a, b):
    return lax.dot_general(a.astype(BF16), b.astype(BF16), (((1,), (1,)), ((), ())),
                           preferred_element_type=F32)


def _dot_tn(a, b):
    return lax.dot_general(a.astype(BF16), b.astype(BF16), (((0,), (0,)), ((), ())),
                           preferred_element_type=F32)


def _rmsnorm_kernel(x_ref, g_ref, o_ref):
    x = x_ref[...]
    y = x * lax.rsqrt(jnp.mean(x * x, axis=-1, keepdims=True) + EPS)
    o_ref[...] = (y * g_ref[...]).astype(o_ref.dtype)


def _rmsnorm(x, g, out_dtype, row_block, first_block=0, n_blocks=None):
    rows, d = x.shape
    if n_blocks is None:
        n_blocks, out_rows = pl.cdiv(rows, row_block), rows
    else:
        out_rows = n_blocks * row_block
    return pl.pallas_call(
        _rmsnorm_kernel,
        grid=(n_blocks,),
        in_specs=[pl.BlockSpec((row_block, d), lambda i: (i + first_block, 0)),
                  pl.BlockSpec((1, d), lambda i: (0, 0))],
        out_specs=pl.BlockSpec((row_block, d), lambda i: (i, 0)),
        out_shape=jax.ShapeDtypeStruct((out_rows, d), out_dtype),
        compiler_params=_params("parallel"),
        name="rmsnorm",
    )(x, g.reshape(1, d).astype(F32))


def _gateup_kernel(x_ref, wg_ref, wu_ref, o_ref):
    x = x_ref[...]
    g = jnp.dot(x, wg_ref[...].astype(BF16), preferred_element_type=F32)
    u = jnp.dot(x, wu_ref[...].astype(BF16), preferred_element_type=F32)
    o_ref[...] = (g * jax.nn.sigmoid(g) * u).astype(o_ref.dtype)


def _gateup(x, wg, wu, tm, tn):
    rows, d = x.shape
    n = wg.shape[1]
    return pl.pallas_call(
        _gateup_kernel,
        grid=(pl.cdiv(rows, tm), n // tn),
        in_specs=[pl.BlockSpec((tm, d), lambda i, j: (i, 0)),
                  pl.BlockSpec((d, tn), lambda i, j: (0, j)),
                  pl.BlockSpec((d, tn), lambda i, j: (0, j))],
        out_specs=pl.BlockSpec((tm, tn), lambda i, j: (i, j)),
        out_shape=jax.ShapeDtypeStruct((rows, n), BF16),
        compiler_params=_params("parallel", "arbitrary"),
        name="ffn_gate_up",
    )(x, wg, wu)


def _mm_resid_kernel(a_ref, w_ref, r_ref, o_ref, *, scale):
    acc = jnp.dot(a_ref[...], w_ref[...].astype(BF16), preferred_element_type=F32)
    o_ref[...] = r_ref[...] + scale * acc


def _mm_resid(a, w, resid, scale, tm, tn):
    rows, k = a.shape
    n = w.shape[1]
    return pl.pallas_call(
        functools.partial(_mm_resid_kernel, scale=scale),
        grid=(pl.cdiv(rows, tm), n // tn),
        in_specs=[pl.BlockSpec((tm, k), lambda i, j: (i, 0)),
                  pl.BlockSpec((k, tn), lambda i, j: (0, j)),
                  pl.BlockSpec((tm, tn), lambda i, j: (i, j))],
        out_specs=pl.BlockSpec((tm, tn), lambda i, j: (i, j)),
        out_shape=jax.ShapeDtypeStruct((rows, n), F32),
        compiler_params=_params("parallel", "arbitrary"),
        name="matmul_residual",
    )(a, w, resid)


def _mm_nt_kernel(a_ref, wt_ref, o_ref):
    o_ref[...] = _dot_nt(a_ref[...], wt_ref[...])


def _mm_nt(a, wt, tm, tn, n=None):
    rows, k = a.shape
    n = wt.shape[0] if n is None else n
    return pl.pallas_call(
        _mm_nt_kernel,
        grid=(pl.cdiv(rows, tm), n // tn),
        in_specs=[pl.BlockSpec((tm, k), lambda i, j: (i, 0)),
                  pl.BlockSpec((tn, k), lambda i, j: (j, 0))],
        out_specs=pl.BlockSpec((tm, tn), lambda i, j: (i, j)),
        out_shape=jax.ShapeDtypeStruct((rows, n), F32),
        compiler_params=_params("parallel", "arbitrary"),
        name="matmul_nt",
    )(a, wt)


def _ffn(h, g, wg, wu, wd):
    rows = h.shape[0]
    xn = _rmsnorm(h, g, BF16, 256)
    hid = _gateup(xn, wg, wu, _row_tile(rows, 1088), _col_tile(wg.shape[1]))
    return _mm_resid(hid, wd, h, 0.5, _row_tile(rows, 544), 512)


def _gate_kernel(z_ref, b_ref, o_ref, *, heads):
    z = z_ref[...] + b_ref[...]
    c = GATE_CAP * jnp.tanh(z / GATE_CAP)
    log_sig = jnp.minimum(c, 0.0) - jnp.log1p(jnp.exp(-jnp.abs(c)))
    lane = lax.broadcasted_iota(jnp.int32, c.shape, 1)
    o_ref[...] = jnp.where(lane < heads, c, log_sig)


def _gates(z, bias, heads):
    rows, d = z.shape
    rb = 512
    return pl.pallas_call(
        functools.partial(_gate_kernel, heads=heads),
        grid=(pl.cdiv(rows, rb),),
        in_specs=[pl.BlockSpec((rb, d), lambda i: (i, 0)), pl.BlockSpec((1, d), lambda i: (0, 0))],
        out_specs=pl.BlockSpec((rb, d), lambda i: (i, 0)),
        out_shape=jax.ShapeDtypeStruct((rows, d), F32),
        compiler_params=_params("parallel"),
        name="mlstm_gates",
    )(z, bias)


def _proj_rope_kernel(a_ref, wt_ref, cos_ref, sin_ref, o0_ref, o1_ref, *, mode):
    z = _dot_nt(a_ref[...], wt_ref[...])
    if mode == "v":
        o0_ref[...] = z
        o1_ref[...] = z.astype(BF16)
        return
    cos = cos_ref[...]
    sin = sin_ref[...]
    lane = lax.broadcasted_iota(jnp.int32, cos.shape, 1)
    first_half = (lane % DIFF_DH) < (DIFF_DH // 2)
    map0 = lane < DIFF_DH
    for j in range(z.shape[1] // LANES):
        sl = slice(j * LANES, (j + 1) * LANES)
        x = z[:, sl]
        partner = jnp.where(first_half, pltpu.roll(x, LANES - DIFF_DH // 2, 1), pltpu.roll(x, DIFF_DH // 2, 1))
        r = x * cos + partner * sin
        if mode == "q":
            r = r * Q_SCALE_LOG2
            o0_ref[:, sl] = jnp.where(map0, r, 0.0).astype(BF16)
            o1_ref[:, sl] = jnp.where(map0, 0.0, r).astype(BF16)
        else:
            o0_ref[:, sl] = r
            o1_ref[:, sl] = r.astype(BF16)


def _proj_rope(a, wt, row0, width, cos, sin, mode, tm, tn):
    rows, k = a.shape
    off = row0 // tn
    tspec = pl.BlockSpec((tm, LANES), lambda i, j: (i, 0))
    ospec = pl.BlockSpec((tm, tn), lambda i, j: (i, j))
    return pl.pallas_call(
        functools.partial(_proj_rope_kernel, mode=mode),
        grid=(pl.cdiv(rows, tm), width // tn),
        in_specs=[pl.BlockSpec((tm, k), lambda i, j: (i, 0)),
                  pl.BlockSpec((tn, k), lambda i, j: (j + off, 0)), tspec, tspec],
        out_specs=[ospec, ospec],
        out_shape=[jax.ShapeDtypeStruct((rows, width), BF16 if mode == "q" else F32),
                   jax.ShapeDtypeStruct((rows, width), BF16)],
        compiler_params=_params("parallel", "arbitrary"),
        name="proj_rope_" + mode,
    )(a, wt, cos, sin)


def _lane_cumsum(x):
    lane = lax.broadcasted_iota(jnp.int32, x.shape, 1)
    k = 1
    while k < x.shape[1]:
        x = x + jnp.where(lane >= k, pltpu.roll(x, k, 1), 0.0)
        k *= 2
    return x


def _mlstm_heads(load, store, c_ref, n_ref, m_ref, heads, valid):
    ins = [load(h) for h in range(heads)]
    t, dk = ins[0][0].shape
    ri = lax.broadcasted_iota(jnp.int32, (t, t), 0)
    ci = lax.broadcasted_iota(jnp.int32, (t, t), 1)
    eye = ri == ci
    lane = lax.broadcasted_iota(jnp.int32, (1, t), 1)

    stage1 = []
    for h, (q, k, v, o, li_row, b_row, norm_row) in enumerate(ins):
        k = k * (dk ** -0.5)
        c_state = c_ref[h]
        stage1.append((k, c_state, _dot_nt(q, k), _dot(q, c_state)))

    stage2 = []
    for h, (q, k_raw, v, o, li_row, b_row, norm_row) in enumerate(ins):
        k, c_state, qk, qc = stage1[h]
        m_prev = m_ref[h:h + 1, 0:1]
        b_mat = jnp.broadcast_to(b_row, (t, t))
        b_col = jnp.sum(jnp.where(eye, b_mat, 0.0), axis=1, keepdims=True)
        log_d = jnp.where(ci <= ri, b_col - b_mat + li_row, -jnp.inf)
        m_inter = m_prev + b_col
        m_t = jnp.maximum(m_inter, jnp.max(log_d, axis=1, keepdims=True))
        w_inter = jnp.exp(m_inter - m_t)
        s = qk * jnp.exp(log_d - m_t)
        m_new = m_t[valid - 1:valid, :]
        decay = jnp.exp(m_inter[valid - 1:valid, :] - m_new)
        wk_row = jnp.where(lane < valid, jnp.exp(b_row[:, valid - 1:valid] - b_row + li_row - m_new), 0.0)
        wk_col = jnp.sum(jnp.where(eye, jnp.broadcast_to(wk_row, (t, t)), 0.0), axis=1, keepdims=True)
        stage2.append((s, k * wk_col, w_inter, m_t, m_new, decay))

    stage3 = []
    for h, (q, k_raw, v, o, li_row, b_row, norm_row) in enumerate(ins):
        s, kw = stage2[h][:2]
        stage3.append((_dot(s, v), _dot_tn(kw, v)))

    for h, (q, k_raw, v, o, li_row, b_row, norm_row) in enumerate(ins):
        k, c_state, qk, qc = stage1[h]
        s, kw, w_inter, m_t, m_new, decay = stage2[h]
        sv, kv = stage3[h]
        n_row = n_ref[h:h + 1, :]
        num = w_inter * qc + sv
        den = w_inter * jnp.sum(q * n_row, axis=1, keepdims=True) + jnp.sum(s, axis=1, keepdims=True)
        y = num / jnp.maximum(jnp.abs(den), jnp.exp(-m_t))
        c_ref[h] = decay * c_state + kv
        n_ref[h:h + 1, :] = decay * n_row + jnp.sum(kw, axis=0, keepdims=True)
        m_ref[h:h + 1, :] = jnp.broadcast_to(m_new, (1, m_ref.shape[1]))
        yn = y * lax.rsqrt(jnp.mean(y * y, axis=-1, keepdims=True) + EPS) * norm_row
        store(h, jax.nn.sigmoid(o) * yn)


def _mlstm_prompt_kernel(q_ref, k_ref, v_ref, o_ref, li_ref, lf_ref,
                         qm_ref, km_ref, vm_ref, om_ref, lim_ref, lfm_ref, norm_ref,
                         out_ref, outm_ref, c_ref, n_ref, m_ref, *, heads, dk, dv, chunks):
    g = pl.program_id(0)

    @pl.when(g == 0)
    def _():
        c_ref[...] = jnp.zeros_like(c_ref)
        n_ref[...] = jnp.zeros_like(n_ref)
        m_ref[...] = jnp.full_like(m_ref, M_INIT)
        bm = _lane_cumsum(lfm_ref[...])

        def load_meta(h):
            return (qm_ref[:, h * dk:(h + 1) * dk], km_ref[:, h * dk:(h + 1) * dk],
                    vm_ref[:, h * dv:(h + 1) * dv], om_ref[:, h * dv:(h + 1) * dv],
                    lim_ref[h:h + 1, 0:N_META], bm[h:h + 1, 0:N_META], norm_ref[:, h * dv:(h + 1) * dv])

        def store_meta(h, y):
            outm_ref[:, h * dv:(h + 1) * dv] = y.astype(outm_ref.dtype)

        _mlstm_heads(load_meta, store_meta, c_ref, n_ref, m_ref, heads, N_META)

    def body(c, carry):
        r0 = pl.multiple_of(c * CHUNK, CHUNK)
        rows = pl.ds(r0, CHUNK)
        gc = g * chunks + c

        def load(h):
            return (q_ref[rows, h * dk:(h + 1) * dk], k_ref[rows, h * dk:(h + 1) * dk],
                    v_ref[rows, h * dv:(h + 1) * dv], o_ref[rows, h * dv:(h + 1) * dv],
                    li_ref[h, pl.ds(gc, 1), :][:, 0:CHUNK], _lane_cumsum(lf_ref[h, pl.ds(gc, 1), :])[:, 0:CHUNK],
                    norm_ref[:, h * dv:(h + 1) * dv])

        def store(h, y):
            out_ref[rows, h * dv:(h + 1) * dv] = y.astype(out_ref.dtype)

        _mlstm_heads(load, store, c_ref, n_ref, m_ref, heads, CHUNK)
        return carry

    lax.fori_loop(0, chunks, body, 0)


def _mlstm_prompt(z, li, lf, li_meta, lf_meta, norm, seq, heads, dk, dv, mix_width):
    chunks = 4
    rb = chunks * CHUNK
    qw, vw = heads * dk, heads * dv
    meta_blk = seq // N_META
    spec = lambda w, c: pl.BlockSpec((rb, w), lambda g, c=c: (g, c))
    mspec = lambda w, c: pl.BlockSpec((N_META, w), lambda g, c=c: (meta_blk, c))
    gspec = pl.BlockSpec((heads, seq // CHUNK, LANES), lambda g: (0, 0, 0))
    const2 = lambda shape: pl.BlockSpec(shape, lambda g: (0,) * len(shape))
    return pl.pallas_call(
        functools.partial(_mlstm_prompt_kernel, heads=heads, dk=dk, dv=dv, chunks=chunks),
        grid=(seq // rb,),
        in_specs=[spec(qw, 0), spec(qw, 1), spec(vw, 1), spec(vw, 2), gspec, gspec,
                  mspec(qw, 0), mspec(qw, 1), mspec(vw, 1), mspec(vw, 2),
                  const2((heads, LANES)), const2((heads, LANES)), const2((1, vw))],
        out_specs=[pl.BlockSpec((rb, vw), lambda g: (g, 0)), const2((N_META, vw)),
                   const2((heads, dk, dv)), const2((heads, dk)), const2((heads, LANES))],
        out_shape=[jax.ShapeDtypeStruct((z.shape[0], mix_width), BF16), jax.ShapeDtypeStruct((N_META, vw), BF16),
                   jax.ShapeDtypeStruct((heads, dk, dv), F32), jax.ShapeDtypeStruct((heads, dk), F32),
                   jax.ShapeDtypeStruct((heads, LANES), F32)],
        compiler_params=_params("arbitrary"),
        name="mlstm_prompt",
    )(z, z, z, z, li, lf, z, z, z, z, li_meta, lf_meta, norm)


def _mlstm_sample_kernel(q_ref, k_ref, v_ref, o_ref, li_ref, lf_ref, c0_ref, n0_ref, m0_ref, norm_ref,
                         out_ref, c_ref, n_ref, m_ref, *, heads, dk, dv, valid):
    c_ref[...] = c0_ref[...]
    n_ref[...] = n0_ref[...]
    m_ref[...] = m0_ref[...]
    t = q_ref.shape[0]
    b_all = _lane_cumsum(lf_ref[...])

    def load(h):
        return (q_ref[:, h * dk:(h + 1) * dk], k_ref[:, h * dk:(h + 1) * dk],
                v_ref[:, h * dv:(h + 1) * dv], o_ref[:, h * dv:(h + 1) * dv],
                li_ref[h:h + 1, 0:t], b_all[h:h + 1, 0:t], norm_ref[:, h * dv:(h + 1) * dv])

    def store(h, y):
        out_ref[:, h * dv:(h + 1) * dv] = y

    _mlstm_heads(load, store, c_ref, n_ref, m_ref, heads, valid)


def _mlstm_sample(zs, li, lf, c0, n0, m0, norm, heads, dk, dv, valid):
    nb, t, _ = zs.shape
    qw, vw = heads * dk, heads * dv
    spec = lambda w, c: pl.BlockSpec((None, t, w), lambda b, c=c: (b, 0, c))
    per_b = lambda *shape: pl.BlockSpec((None,) + shape, lambda b: (b,) + (0,) * len(shape))
    return pl.pallas_call(
        functools.partial(_mlstm_sample_kernel, heads=heads, dk=dk, dv=dv, valid=valid),
        grid=(nb,),
        in_specs=[spec(qw, 0), spec(qw, 1), spec(vw, 1), spec(vw, 2), per_b(heads, LANES), per_b(heads, LANES),
                  per_b(heads, dk, dv), per_b(heads, dk), per_b(heads, LANES),
                  pl.BlockSpec((1, vw), lambda b: (0, 0))],
        out_specs=[per_b(t, vw), per_b(heads, dk, dv), per_b(heads, dk), per_b(heads, LANES)],
        out_shape=[jax.ShapeDtypeStruct((nb, t, vw), F32), jax.ShapeDtypeStruct((nb, heads, dk, dv), F32),
                   jax.ShapeDtypeStruct((nb, heads, dk), F32), jax.ShapeDtypeStruct((nb, heads, LANES), F32)],
        compiler_params=_params("parallel"),
        name="mlstm_sample",
    )(zs, zs, zs, zs, li, lf, c0, n0, m0, norm)


def _lambda_value(lp_ref, lam_init):
    lp = lp_ref[...]
    a = jnp.sum(lp[0:1, :] * lp[1:2, :], axis=1, keepdims=True)
    b = jnp.sum(lp[2:3, :] * lp[3:4, :], axis=1, keepdims=True)
    return jnp.exp(a) - jnp.exp(b) + lam_init


def _scores_t(q, kb, mask):
    s = _dot_nt(kb, q)
    return s if mask is None else jnp.where(mask, s, MASK_VALUE)


def _softmax_step_t(q, kb, vb, m_ref, l_ref, acc_ref, mask):
    _softmax_update_t(_scores_t(q, kb, mask), vb, m_ref, l_ref, acc_ref)


def _softmax_update_t(s, vb, m_ref, l_ref, acc_ref, cols=slice(None)):
    m_old = m_ref[:, cols]
    m_new = jnp.maximum(m_old, jnp.max(s, axis=0, keepdims=True))
    alpha = jnp.exp2(m_old - m_new)
    p = jnp.exp2(s - m_new)
    l_ref[:, cols] = alpha * l_ref[:, cols] + jnp.sum(p, axis=0, keepdims=True)
    acc_ref[:, cols] = alpha * acc_ref[:, cols] + _dot_tn(vb, p)
    m_ref[:, cols] = m_new


def _diff_finish_t(lam, sub_ref, l0_ref, a0_ref, l1_ref, a1_ref):
    o_t = a0_ref[...] / l0_ref[...] - lam * (a1_ref[...] / l1_ref[...])
    o = o_t.T
    return o * lax.rsqrt(jnp.mean(o * o, axis=-1, keepdims=True) + EPS) * sub_ref[...]


def _init_softmax_state(m_ref, l_ref, acc_ref):
    m_ref[...] = jnp.full_like(m_ref, MASK_VALUE)
    l_ref[...] = jnp.zeros_like(l_ref)
    acc_ref[...] = jnp.zeros_like(acc_ref)


def _flash_kernel(q0_ref, q1_ref, k_ref, v_ref, lp_ref, sub_ref, mix_ref, out_ref, *state, seq, tq, tk, lam_init):
    del mix_ref
    qi = pl.program_id(1)
    groups = tq // tk
    chains = [state[6 * g:6 * g + 6] for g in range(groups)]
    km, vm = k_ref[seq:seq + N_META, :], v_ref[seq:seq + N_META, :]
    queries = []
    for g, (m0, l0, a0, m1, l1, a1) in enumerate(chains):
        q0 = q0_ref[g * tk:(g + 1) * tk, :]
        q1 = q1_ref[g * tk:(g + 1) * tk, :]
        queries.append((q0, q1))
        _init_softmax_state(m0, l0, a0)
        _init_softmax_state(m1, l1, a1)
        _softmax_step_t(q0, km, vm, m0, l0, a0, None)
        _softmax_step_t(q1, km, vm, m1, l1, a1, None)

    ki = lax.broadcasted_iota(jnp.int32, (tk, tk), 0)
    qj = lax.broadcasted_iota(jnp.int32, (tk, tk), 1)
    causal = ki <= qj

    def step(j, first_group, diagonal):
        rows = pl.ds(pl.multiple_of(j * tk, tk), tk)
        kb, vb = k_ref[rows, :], v_ref[rows, :]
        scores = []
        for g in range(first_group, groups):
            mask = causal if (diagonal and g == first_group) else None
            scores.append((_scores_t(queries[g][0], kb, mask), _scores_t(queries[g][1], kb, mask)))
        for g, (s0, s1) in zip(range(first_group, groups), scores):
            m0, l0, a0, m1, l1, a1 = chains[g]
            _softmax_update_t(s0, vb, m0, l0, a0)
            _softmax_update_t(s1, vb, m1, l1, a1)

    def body(j, carry):
        step(j, 0, False)
        return carry

    lax.fori_loop(0, qi * groups, body, 0)
    for g in range(groups):
        step(qi * groups + g, g, True)

    lam = _lambda_value(lp_ref, lam_init)
    for g, (m0, l0, a0, m1, l1, a1) in enumerate(chains):
        out_ref[g * tk:(g + 1) * tk, :] = _diff_finish_t(lam, sub_ref, l0, a0, l1, a1).astype(out_ref.dtype)


def _flash(q0, q1, kb, vb, lam_p, subln, mix, seq, lam_init):
    width = q0.shape[1]
    heads = width // DIFF_DV
    col0 = (mix.shape[1] - width) // DIFF_DV
    tq = min(FLASH_Q_TILE, seq)
    tk = min(FLASH_K_TILE, tq)
    kv_rows = seq + N_META
    qspec = pl.BlockSpec((tq, DIFF_DV), lambda h, i: (i, h))
    kspec = pl.BlockSpec((kv_rows, DIFF_DV), lambda h, i: (0, h))
    stat = pltpu.VMEM((1, tk), F32)
    acc = pltpu.VMEM((DIFF_DV, tk), F32)
    return pl.pallas_call(
        functools.partial(_flash_kernel, seq=seq, tq=tq, tk=tk, lam_init=lam_init),
        grid=(heads, seq // tq),
        in_specs=[qspec, qspec, kspec, kspec,
                  pl.BlockSpec(lam_p.shape, lambda h, i: (0, 0)), pl.BlockSpec((1, DIFF_DV), lambda h, i: (0, 0)),
                  pl.BlockSpec(memory_space=pl.ANY)],
        out_specs=pl.BlockSpec((tq, DIFF_DV), lambda h, i: (i, col0 + h)),
        out_shape=jax.ShapeDtypeStruct(mix.shape, mix.dtype),
        input_output_aliases={6: 0},
        scratch_shapes=[stat, stat, acc, stat, stat, acc] * (tq // tk),
        compiler_params=_params("parallel", "arbitrary"),
        name="diff_attn_prompt",
    )(q0, q1, kb, vb, lam_p, subln, mix)


def _meta_attn_kernel(q0_ref, q1_ref, k_ref, v_ref, lp_ref, sub_ref, out_ref,
                      m0, l0, a0, m1, l1, a1, *, lam_init):
    _init_softmax_state(m0, l0, a0)
    _init_softmax_state(m1, l1, a1)
    t = q0_ref.shape[0]
    ki = lax.broadcasted_iota(jnp.int32, (t, t), 0)
    qj = lax.broadcasted_iota(jnp.int32, (t, t), 1)
    mask = ki <= qj
    _softmax_step_t(q0_ref[...], k_ref[...], v_ref[...], m0, l0, a0, mask)
    _softmax_step_t(q1_ref[...], k_ref[...], v_ref[...], m1, l1, a1, mask)
    out_ref[...] = _diff_finish_t(_lambda_value(lp_ref, lam_init), sub_ref, l0, a0, l1, a1)


def _meta_attn(q0, q1, kb, vb, lam_p, subln, lam_init):
    t, width = q0.shape
    heads = width // DIFF_DV
    spec = pl.BlockSpec((t, DIFF_DV), lambda h: (0, h))
    stat = pltpu.VMEM((1, t), F32)
    acc = pltpu.VMEM((DIFF_DV, t), F32)
    return pl.pallas_call(
        functools.partial(_meta_attn_kernel, lam_init=lam_init),
        grid=(heads,),
        in_specs=[spec, spec, spec, spec,
                  pl.BlockSpec(lam_p.shape, lambda h: (0, 0)), pl.BlockSpec((1, DIFF_DV), lambda h: (0, 0))],
        out_specs=spec,
        out_shape=jax.ShapeDtypeStruct((t, width), F32),
        scratch_shapes=[stat, stat, acc, stat, stat, acc],
        compiler_params=_params("parallel"),
        name="diff_attn_meta",
    )(q0, q1, kb, vb, lam_p, subln)


def _sample_attn_kernel(pt_ref, q_ref, *refs, pages, heads, new_valid, q_len, lam_init):
    k_refs = refs[:pages]
    v_refs = refs[pages:2 * pages]
    kn_ref, vn_ref, lp_ref, sub_ref, out_ref, kcat, vcat, m_s, l_s, acc_s = refs[2 * pages:]
    j = pl.program_id(1)
    page = k_refs[0].shape[1]
    keys = PAGES_PER_UPDATE * page
    rows_q = q_ref.shape[0]
    group = rows_q // heads

    @pl.when(j == 0)
    def _():
        m_s[...] = jnp.full_like(m_s, MASK_VALUE)
        l_s[...] = jnp.zeros_like(l_s)
        acc_s[...] = jnp.zeros_like(acc_s)

    q = q_ref[...]

    def update(k_t, weighted_values, mask):
        s = jnp.dot(q, k_t, preferred_element_type=F32)
        if mask is not None:
            s = jnp.where(mask, s, MASK_VALUE)
        m_old = m_s[...]
        m_new = jnp.maximum(m_old, jnp.max(s, axis=1, keepdims=True))
        alpha = jnp.exp2(m_old - m_new)
        p = jnp.exp2(s - m_new)
        l_s[...] = alpha * l_s[...] + jnp.sum(p, axis=1, keepdims=True)
        acc_s[...] = alpha * acc_s[...] + weighted_values(p)
        m_s[...] = m_new

    def cache_values(p):
        own_head = (lax.broadcasted_iota(jnp.int32, (heads, rows_q), 0)
                    == lax.broadcasted_iota(jnp.int32, (heads, rows_q), 1) // group)
        p_t = p.T
        expanded = jnp.where(own_head[None], p_t[:, None, :], 0.0).reshape(keys * heads, rows_q)
        return _dot_tn(expanded, vcat[...])

    def new_values(p):
        return jnp.concatenate(
            [jnp.dot(p[h * group:(h + 1) * group, :].astype(BF16), vn_ref[:, h * DIFF_DV:(h + 1) * DIFF_DV],
                     preferred_element_type=F32) for h in range(heads)], axis=0)

    for first in range(0, pages, PAGES_PER_UPDATE):
        for i in range(PAGES_PER_UPDATE):
            kcat[:, i * page:(i + 1) * page] = k_refs[first + i][...].astype(BF16)
            vcat[i * page * heads:(i + 1) * page * heads, :] = v_refs[first + i][...].astype(BF16)
        update(kcat[...], cache_values, None)

    @pl.when(j == pl.num_programs(1) - 1)
    def _():
        n_new = kn_ref.shape[1]
        r = lax.broadcasted_iota(jnp.int32, (rows_q, n_new), 0)
        u = lax.broadcasted_iota(jnp.int32, (rows_q, n_new), 1)
        update(kn_ref[...], new_values, (u <= r % q_len) & (u < new_valid))
        on = acc_s[...] / l_s[...]
        lam = _lambda_value(lp_ref, lam_init)
        d = on - lam * pltpu.roll(on, rows_q - q_len, 0)
        out_ref[...] = d * lax.rsqrt(jnp.mean(d * d, axis=-1, keepdims=True) + EPS) * sub_ref[...]


def _sample_attn(page_table, qbd, cache_kt, cache_v, k_new_t, v_new, lam_p, subln, q_len, lam_init):
    nb, rows_q, width = qbd.shape
    heads = width // DIFF_DV
    page = cache_kt.shape[2]
    n_pages = page_table.shape[1]
    pages = PAGES_PER_STEP
    k_spec = lambda i: pl.BlockSpec((None, width, page), lambda b, j, pt, i=i: (pt[b, j * pages + i], 0, 0))
    v_spec = lambda i: pl.BlockSpec((None, page * heads, DIFF_DV),
                                    lambda b, j, pt, i=i: (pt[b, j * pages + i], 0, 0))
    per_b = lambda shape: pl.BlockSpec((None,) + shape, lambda b, j, pt: (b, 0, 0))
    grid_spec = pltpu.PrefetchScalarGridSpec(
        num_scalar_prefetch=1,
        grid=(nb, n_pages // pages),
        in_specs=[per_b((rows_q, width))] + [k_spec(i) for i in range(pages)] + [v_spec(i) for i in range(pages)]
                 + [per_b(k_new_t.shape[1:]), per_b(v_new.shape[1:]),
                    pl.BlockSpec(lam_p.shape, lambda b, j, pt: (0, 0)),
                    pl.BlockSpec((1, DIFF_DV), lambda b, j, pt: (0, 0))],
        out_specs=pl.BlockSpec((None, rows_q, DIFF_DV), lambda b, j, pt: (b, 0, 0)),
        scratch_shapes=[pltpu.VMEM((width, PAGES_PER_UPDATE * page), BF16),
                        pltpu.VMEM((heads * PAGES_PER_UPDATE * page, DIFF_DV), BF16),
                        pltpu.VMEM((rows_q, 1), F32), pltpu.VMEM((rows_q, 1), F32),
                        pltpu.VMEM((rows_q, DIFF_DV), F32)],
    )
    return pl.pallas_call(
        functools.partial(_sample_attn_kernel, pages=pages, heads=heads, new_valid=q_len, q_len=q_len,
                          lam_init=lam_init),
        grid_spec=grid_spec,
        out_shape=jax.ShapeDtypeStruct((nb, rows_q, DIFF_DV), F32),
        compiler_params=_params("parallel", "arbitrary"),
        name="diff_attn_sample",
    )(page_table, qbd, *([cache_kt] * pages), *([cache_v] * pages), k_new_t, v_new, lam_p, subln)


def _rope_tables(pos):
    half = DIFF_DH // 2
    inv = ROPE_THETA ** (-jnp.arange(half, dtype=F32) / half)
    ang = jnp.asarray(pos, F32)[:, None] * inv[None, :]
    cos, sin = jnp.cos(ang), jnp.sin(ang)
    cos = jnp.tile(cos, (1, LANES // half))
    sin = jnp.tile(jnp.concatenate([-sin, sin], axis=1), (1, LANES // DIFF_DH))
    return cos, sin


def kernel(x_prompt, x_sample, cache_k, cache_v, state_C, state_n, state_m, page_table, meta_tokens,
           norm_ffn1, ffn1_w_gate, ffn1_w_up, ffn1_w_down, norm_mix, w_in, mlstm_b_i, mlstm_b_f,
           mlstm_norm, diff_lambda_qk, diff_subln, w_out, norm_ffn2, ffn2_w_gate, ffn2_w_up,
           ffn2_w_down, norm_final):
    batch, seq, d_model = x_prompt.shape
    assert batch == 1 and state_C.shape[0] == 1
    dec_b, dec_t, _ = x_sample.shape
    heads_m = mlstm_b_i.shape[1]
    dk, dv = state_C.shape[3], state_C.shape[4]
    qw, vw = heads_m * dk, heads_m * dv
    dw = w_out.shape[1] - vw
    heads_d = dw // DIFF_DV
    assert vw == 2 * qw and dw == vw, "column blocks of the projection are addressed in units of the q width"
    n_s = dec_b * dec_t
    rows = seq + N_META + n_s
    past = page_table.shape[1] * cache_k.shape[2]
    lam_init = 0.8 - 0.6 * math.exp(-0.0)

    def ffn_weights(wg, wu, wd):
        return wg, wu, wd.astype(BF16)

    h = jnp.concatenate([x_prompt[0], meta_tokens.astype(F32), x_sample.reshape(n_s, d_model)], axis=0)

    h = _ffn(h, norm_ffn1[0], *ffn_weights(ffn1_w_gate[0], ffn1_w_up[0], ffn1_w_down[0]))

    wt = w_in[0].T
    c_gate = 2 * qw + 2 * vw
    wt_gate = jnp.pad(wt[c_gate:c_gate + 2 * heads_m], ((0, LANES - 2 * heads_m), (0, 0)))
    u = _rmsnorm(h, norm_mix[0], BF16, 256)
    tm = _row_tile(rows, 1088)
    z = _mm_nt(u, wt, tm, _col_tile(c_gate), c_gate)
    zg = _mm_nt(u, wt_gate, tm, LANES)
    pos = np.concatenate([N_META + np.arange(seq), np.arange(N_META), past + np.tile(np.arange(dec_t), dec_b)])
    cos, sin = _rope_tables(pos)
    wt_d = wt[c_gate + 2 * heads_m:].astype(BF16)
    tn_d = _col_tile(dw)
    q0, q1 = _proj_rope(u, wt_d, 0, dw, cos, sin, "q", tm, tn_d)
    k_rot, kb = _proj_rope(u, wt_d, dw, dw, cos, sin, "k", tm, tn_d)
    v_all, vb = _proj_rope(u, wt_d, 2 * dw, dw, cos, sin, "v", tm, tn_d)
    gate_bias = jnp.pad(jnp.concatenate([mlstm_b_i[0], mlstm_b_f[0]]), (0, LANES - 2 * heads_m)).reshape(1, LANES)
    gates = _gates(zg, gate_bias.astype(F32), heads_m)
    log_i, log_f = gates[:, :heads_m], gates[:, heads_m:2 * heads_m]

    def chunk_rows(a):
        a = a.T.reshape(heads_m, seq // CHUNK, CHUNK)
        return jnp.pad(a, ((0, 0), (0, 0), (0, LANES - CHUNK)))

    lane_rows = lambda a: jnp.pad(a, ((0, 0),) * (a.ndim - 1) + ((0, LANES - a.shape[-1]),))
    norm_m = mlstm_norm[0].reshape(1, vw).astype(F32)
    mix, mo_meta, c_p, n_p, m_p = _mlstm_prompt(
        z, chunk_rows(log_i[:seq]), chunk_rows(log_f[:seq]),
        lane_rows(log_i[seq:seq + N_META].T), lane_rows(log_f[seq:seq + N_META].T), norm_m, seq, heads_m, dk, dv,
        vw + dw)

    t_pad = 8
    zs = jnp.pad(z[seq + N_META:].reshape(dec_b, dec_t, c_gate), ((0, 0), (0, t_pad - dec_t), (0, 0)))
    gate_s = lambda a: lane_rows(a[seq + N_META:].reshape(dec_b, dec_t, heads_m).transpose(0, 2, 1))
    m0 = jnp.broadcast_to(state_m[0][:, :, None], (dec_b, heads_m, LANES))
    mo_s, c_s, n_s_new, m_s = _mlstm_sample(zs, gate_s(log_i), gate_s(log_f), state_C[0], state_n[0], m0, norm_m,
                                             heads_m, dk, dv, dec_t)

    sub =(diff_subln[0].astype(F32) * (1.0 - lam_init)).reshape(1, DIFF_DV)
    lam_p = diff_lambda_qk[0].astype(F32)
    mix = _flash(q0, q1, kb, vb, lam_p, sub, mix, seq, lam_init)

    pad128 = lambda a: jnp.pad(a, ((0, LANES - a.shape[0]), (0, 0)))
    meta = slice(seq, seq + N_META)
    do_meta = _meta_attn(pad128(q0[meta]), pad128(q1[meta]), pad128(kb[meta]), pad128(vb[meta]), lam_p, sub,
                         lam_init)[:N_META]

    qs = (q0[seq + N_META:] + q1[seq + N_META:]).reshape(dec_b, 1, dec_t, dw)
    rows_q = heads_d * 2 * dec_t
    qbd = jnp.broadcast_to(qs, (dec_b, heads_d * 2, dec_t, dw)).reshape(dec_b, rows_q, dw)
    r_id = np.arange(rows_q)[:, None] // dec_t
    c_id = np.arange(dw)[None, :] // DIFF_DH
    qbd = jnp.where(jnp.asarray(r_id == c_id), qbd, jnp.zeros_like(qbd))
    new_rows = lambda a: jnp.pad(a[seq + N_META:].reshape(dec_b, dec_t, dw), ((0, 0), (0, LANES - dec_t), (0, 0)))
    n_phys, page = cache_k.shape[1], cache_k.shape[2]
    cache_kt = jnp.transpose(cache_k[0], (0, 2, 3, 4, 1)).reshape(n_phys, dw, page)
    do_s = _sample_attn(page_table, qbd, cache_kt, cache_v[0].reshape(n_phys, page * heads_d, DIFF_DV),
                        new_rows(kb).transpose(0, 2, 1), new_rows(vb), lam_p, sub, dec_t, lam_init)
    do_s = do_s.reshape(dec_b, heads_d, 2, dec_t, DIFF_DV)[:, :, 0].transpose(0, 2, 1, 3).reshape(n_s, dw)

    tail = jnp.concatenate([
        jnp.concatenate([mo_meta, mo_s[:, :dec_t].reshape(n_s, vw).astype(BF16)], axis=0),
        jnp.concatenate([do_meta.astype(BF16), do_s.astype(BF16)], axis=0)], axis=1)
    mix = lax.dynamic_update_slice(mix, tail, (seq, 0))
    h = _mm_resid(mix, w_out[0], h, 1.0, _row_tile(rows, 1088), 512)

    h = _ffn(h, norm_ffn2[0], *ffn_weights(ffn2_w_gate[0], ffn2_w_up[0], ffn2_w_down[0]))

    y_prompt = _rmsnorm(h, norm_final, F32, 256, 0, seq // 256).reshape(1, seq, d_model)
    y_sample = _rmsnorm(h, norm_final, F32, N_META, (seq + N_META) // N_META, n_s // N_META)
    y_sample = y_sample.reshape(dec_b, dec_t, d_model)

    order = lambda a: jnp.concatenate([a[meta], a[:seq]], axis=0)
    k_prompt = order(k_rot).reshape(1, 1, seq + N_META, heads_d, 2, DIFF_DH)
    v_prompt = order(v_all).reshape(1, 1, seq + N_META, heads_d, DIFF_DV)
    k_sample = k_rot[seq + N_META:].reshape(1, dec_b, dec_t, heads_d, 2, DIFF_DH)
    v_sample = v_all[seq + N_META:].reshape(1, dec_b, dec_t, heads_d, DIFF_DV)
    return (y_prompt, y_sample, k_prompt, v_prompt,
            c_p[None, None], n_p[None, None], m_p[None, None, :, 0],
            k_sample, v_sample, c_s[None], n_s_new[None], m_s[None, :, :, 0])
```

```python
import functools
import math

import numpy as np
import jax
import jax.numpy as jnp
from jax import lax
from jax.experimental import pallas as pl
from jax.experimental.pallas import tpu as pltpu

F32 = jnp.float32
BF16 = jnp.bfloat16

N_META = 16
CHUNK = 64
ROPE_THETA = 10000.0
GATE_CAP = 15.0
M_INIT = -1e30
EPS = 1e-6
DIFF_DV = 128
DIFF_DH = DIFF_DV // 2
DIFF_SCALE = DIFF_DH ** -0.5
Q_SCALE_LOG2 = DIFF_SCALE * math.log2(math.e)
PAGES_PER_STEP = 8
PAGES_PER_UPDATE = 4
FLASH_Q_TILE = 2048
FLASH_K_TILE = 512
MASK_VALUE = -1e30

LANES = 128
V7X_VMEM_BYTES = 64 * 2 ** 20
VMEM_LIMIT = V7X_VMEM_BYTES - 8 * 2 ** 20


def _params(*sem):
    return pltpu.CompilerParams(dimension_semantics=sem, vmem_limit_bytes=VMEM_LIMIT)


def _round_up(x, m):
    return (x + m - 1) // m * m


def _row_tile(rows, target):
    n = pl.cdiv(rows, target)
    return _round_up(pl.cdiv(rows, n), 16)


def _col_tile(n):
    return 512 if n % 512 == 0 else 256


def _dot(a, b):
    return jnp.dot(a.astype(BF16), b.astype(BF16), preferred_element_type=F32)


def _dot_nt(a, b):
    return lax.dot_general(a.astype(BF16), b.astype(BF16), (((1,), (1,)), ((), ())),
                           preferred_element_type=F32)


def _dot_tn(a, b):
    return lax.dot_general(a.astype(BF16), b.astype(BF16), (((0,), (0,)), ((), ())),
                           preferred_element_type=F32)


def _rmsnorm_kernel(x_ref, g_ref, o_ref):
    x = x_ref[...]
    y = x * lax.rsqrt(jnp.mean(x * x, axis=-1, keepdims=True) + EPS)
    o_ref[...] = (y * g_ref[...]).astype(o_ref.dtype)


def _rmsnorm(x, g, out_dtype, row_block, first_block=0, n_blocks=None):
    rows, d = x.shape
    if n_blocks is None:
        n_blocks, out_rows = pl.cdiv(rows, row_block), rows
    else:
        out_rows = n_blocks * row_block
    return pl.pallas_call(
        _rmsnorm_kernel,
        grid=(n_blocks,),
        in_specs=[pl.BlockSpec((row_block, d), lambda i: (i + first_block, 0)),
                  pl.BlockSpec((1, d), lambda i: (0, 0))],
        out_specs=pl.BlockSpec((row_block, d), lambda i: (i, 0)),
        out_shape=jax.ShapeDtypeStruct((out_rows, d), out_dtype),
        compiler_params=_params("parallel"),
        name="rmsnorm",
    )(x, g.reshape(1, d).astype(F32))


def _gateup_kernel(x_ref, wg_ref, wu_ref, o_ref):
    x = x_ref[...]
    g = jnp.dot(x, wg_ref[...].astype(BF16), preferred_element_type=F32)
    u = jnp.dot(x, wu_ref[...].astype(BF16), preferred_element_type=F32)
    o_ref[...] = (g * jax.nn.sigmoid(g) * u).astype(o_ref.dtype)


def _gateup(x, wg, wu, tm, tn):
    rows, d = x.shape
    n = wg.shape[1]
    return pl.pallas_call(
        _gateup_kernel,
        grid=(pl.cdiv(rows, tm), n // tn),
        in_specs=[pl.BlockSpec((tm, d), lambda i, j: (i, 0)),
                  pl.BlockSpec((d, tn), lambda i, j: (0, j)),
                  pl.BlockSpec((d, tn), lambda i, j: (0, j))],
        out_specs=pl.BlockSpec((tm, tn), lambda i, j: (i, j)),
        out_shape=jax.ShapeDtypeStruct((rows, n), BF16),
        compiler_params=_params("parallel", "arbitrary"),
        name="ffn_gate_up",
    )(x, wg, wu)


def _mm_resid_kernel(a_ref, w_ref, r_ref, o_ref, *, scale):
    acc = jnp.dot(a_ref[...], w_ref[...].astype(BF16), preferred_element_type=F32)
    o_ref[...] = r_ref[...] + scale * acc


def _mm_resid(a, w, resid, scale, tm, tn):
    rows, k = a.shape
    n = w.shape[1]
    return pl.pallas_call(
        functools.partial(_mm_resid_kernel, scale=scale),
        grid=(pl.cdiv(rows, tm), n // tn),
        in_specs=[pl.BlockSpec((tm, k), lambda i, j: (i, 0)),
                  pl.BlockSpec((k, tn), lambda i, j: (0, j)),
                  pl.BlockSpec((tm, tn), lambda i, j: (i, j))],
        out_specs=pl.BlockSpec((tm, tn), lambda i, j: (i, j)),
        out_shape=jax.ShapeDtypeStruct((rows, n), F32),
        compiler_params=_params("parallel", "arbitrary"),
        name="matmul_residual",
    )(a, w, resid)


def _mm_nt_kernel(a_ref, wt_ref, o_ref):
    o_ref[...] = _dot_nt(a_ref[...], wt_ref[...])


def _mm_nt(a, wt, tm, tn, n=None):
    rows, k = a.shape
    n = wt.shape[0] if n is None else n
    return pl.pallas_call(
        _mm_nt_kernel,
        grid=(pl.cdiv(rows, tm), n // tn),
        in_specs=[pl.BlockSpec((tm, k), lambda i, j: (i, 0)),
                  pl.BlockSpec((tn, k), lambda i, j: (j, 0))],
        out_specs=pl.BlockSpec((tm, tn), lambda i, j: (i, j)),
        out_shape=jax.ShapeDtypeStruct((rows, n), F32),
        compiler_params=_params("parallel", "arbitrary"),
        name="matmul_nt",
    )(a, wt)


def _ffn(h, g, wg, wu, wd):
    rows = h.shape[0]
    xn = _rmsnorm(h, g, BF16, 256)
    hid = _gateup(xn, wg, wu, _row_tile(rows, 1088), _col_tile(wg.shape[1]))
    return _mm_resid(hid, wd, h, 0.5, _row_tile(rows, 544), 512)


def _gate_kernel(z_ref, b_ref, o_ref, *, heads):
    z = z_ref[...] + b_ref[...]
    c = GATE_CAP * jnp.tanh(z / GATE_CAP)
    log_sig = jnp.minimum(c, 0.0) - jnp.log1p(jnp.exp(-jnp.abs(c)))
    lane = lax.broadcasted_iota(jnp.int32, c.shape, 1)
    o_ref[...] = jnp.where(lane < heads, c, log_sig)


def _gates(z, bias, heads):
    rows, d = z.shape
    rb = 512
    return pl.pallas_call(
        functools.partial(_gate_kernel, heads=heads),
        grid=(pl.cdiv(rows, rb),),
        in_specs=[pl.BlockSpec((rb, d), lambda i: (i, 0)), pl.BlockSpec((1, d), lambda i: (0, 0))],
        out_specs=pl.BlockSpec((rb, d), lambda i: (i, 0)),
        out_shape=jax.ShapeDtypeStruct((rows, d), F32),
        compiler_params=_params("parallel"),
        name="mlstm_gates",
    )(z, bias)


def _proj_rope_kernel(a_ref, wt_ref, cos_ref, sin_ref, q0_ref, q1_ref, kf_ref, kb_ref, vf_ref, vb_ref, *, blocks):
    j = pl.program_id(1)
    z = _dot_nt(a_ref[...], wt_ref[...])
    cos = cos_ref[...]
    sin = sin_ref[...]
    lane = lax.broadcasted_iota(jnp.int32, cos.shape, 1)
    first_half = (lane % DIFF_DH) < (DIFF_DH // 2)
    map0 = lane < DIFF_DH

    def rotated(sl):
        x = z[:, sl]
        partner = jnp.where(first_half, pltpu.roll(x, LANES - DIFF_DH // 2, 1), pltpu.roll(x, DIFF_DH // 2, 1))
        return x * cos + partner * sin

    lane_blocks = [slice(c * LANES, (c + 1) * LANES) for c in range(z.shape[1] // LANES)]

    @pl.when(j < blocks)
    def _():
        for sl in lane_blocks:
            r = rotated(sl) * Q_SCALE_LOG2
            q0_ref[:, sl] = jnp.where(map0, r, 0.0).astype(BF16)
            q1_ref[:, sl] = jnp.where(map0, 0.0, r).astype(BF16)

    @pl.when((j >= blocks) & (j < 2 * blocks))
    def _():
        for sl in lane_blocks:
            r = rotated(sl)
            kf_ref[:, sl] = r
            kb_ref[:, sl] = r.astype(BF16)

    @pl.when(j >= 2 * blocks)
    def _():
        vf_ref[...] = z
        vb_ref[...] = z.astype(BF16)


def _proj_rope(a, wt, width, cos, sin, tm, tn):
    rows, k = a.shape
    blocks = width // tn
    tspec = pl.BlockSpec((tm, LANES), lambda i, j: (i, 0))
    ospec = lambda g: pl.BlockSpec((tm, tn), lambda i, j, g=g: (i, jnp.clip(j - g * blocks, 0, blocks - 1)))
    wide, narrow = jax.ShapeDtypeStruct((rows, width), F32), jax.ShapeDtypeStruct((rows, width), BF16)
    return pl.pallas_call(
        functools.partial(_proj_rope_kernel, blocks=blocks),
        grid=(pl.cdiv(rows, tm), 3 * blocks),
        in_specs=[pl.BlockSpec((tm, k), lambda i, j: (i, 0)),
                  pl.BlockSpec((tn, k), lambda i, j: (j, 0)), tspec, tspec],
        out_specs=[ospec(0), ospec(0), ospec(1), ospec(1), ospec(2), ospec(2)],
        out_shape=[narrow, narrow, wide, narrow, wide, narrow],
        compiler_params=_params("parallel", "arbitrary"),
        name="proj_rope",
    )(a, wt, cos, sin)


def _lane_cumsum(x):
    lane = lax.broadcasted_iota(jnp.int32, x.shape, 1)
    k = 1
    while k < x.shape[1]:
        x = x + jnp.where(lane >= k, pltpu.roll(x, k, 1), 0.0)
        k *= 2
    return x


def _mlstm_heads(load, store, c_ref, n_ref, m_ref, heads, valid):
    ins = [load(h) for h in range(heads)]
    t, dk = ins[0][0].shape
    ri = lax.broadcasted_iota(jnp.int32, (t, t), 0)
    ci = lax.broadcasted_iota(jnp.int32, (t, t), 1)
    eye = ri == ci
    lane = lax.broadcasted_iota(jnp.int32, (1, t), 1)

    stage1 = []
    for h, (q, k, v, o, li_row, b_row, norm_row) in enumerate(ins):
        k = k * (dk ** -0.5)
        c_state = c_ref[h]
        stage1.append((k, c_state, _dot_nt(q, k), _dot(q, c_state)))

    stage2 = []
    for h, (q, k_raw, v, o, li_row, b_row, norm_row) in enumerate(ins):
        k, c_state, qk, qc = stage1[h]
        m_prev = m_ref[h:h + 1, 0:1]
        b_mat = jnp.broadcast_to(b_row, (t, t))
        b_col = jnp.sum(jnp.where(eye, b_mat, 0.0), axis=1, keepdims=True)
        log_d = jnp.where(ci <= ri, b_col - b_mat + li_row, -jnp.inf)
        m_inter = m_prev + b_col
        m_t = jnp.maximum(m_inter, jnp.max(log_d, axis=1, keepdims=True))
        w_inter = jnp.exp(m_inter - m_t)
        s = qk * jnp.exp(log_d - m_t)
        m_new = m_t[valid - 1:valid, :]
        decay = jnp.exp(m_inter[valid - 1:valid, :] - m_new)
        wk_row = jnp.where(lane < valid, jnp.exp(b_row[:, valid - 1:valid] - b_row + li_row - m_new), 0.0)
        wk_col = jnp.sum(jnp.where(eye, jnp.broadcast_to(wk_row, (t, t)), 0.0), axis=1, keepdims=True)
        stage2.append((s, k * wk_col, w_inter, m_t, m_new, decay))

    stage3 = []
    for h, (q, k_raw, v, o, li_row, b_row, norm_row) in enumerate(ins):
        s, kw = stage2[h][:2]
        stage3.append((_dot(s, v), _dot_tn(kw, v)))

    for h, (q, k_raw, v, o, li_row, b_row, norm_row) in enumerate(ins):
        k, c_state, qk, qc = stage1[h]
        s, kw, w_inter, m_t, m_new, decay = stage2[h]
        sv, kv = stage3[h]
        n_row = n_ref[h:h + 1, :]
        num = w_inter * qc + sv
        den = w_inter * jnp.sum(q * n_row, axis=1, keepdims=True) + jnp.sum(s, axis=1, keepdims=True)
        y = num / jnp.maximum(jnp.abs(den), jnp.exp(-m_t))
        c_ref[h] = decay * c_state + kv
        n_ref[h:h + 1, :] = decay * n_row + jnp.sum(kw, axis=0, keepdims=True)
        m_ref[h:h + 1, :] = jnp.broadcast_to(m_new, (1, m_ref.shape[1]))
        yn = y * lax.rsqrt(jnp.mean(y * y, axis=-1, keepdims=True) + EPS) * norm_row
        store(h, jax.nn.sigmoid(o) * yn)


def _mlstm_prompt_kernel(q_ref, k_ref, v_ref, o_ref, li_ref, lf_ref,
                         qm_ref, km_ref, vm_ref, om_ref, lim_ref, lfm_ref, norm_ref,
                         out_ref, outm_ref, c_ref, n_ref, m_ref, *, heads, dk, dv, chunks):
    g = pl.program_id(0)

    @pl.when(g == 0)
    def _():
        c_ref[...] = jnp.zeros_like(c_ref)
        n_ref[...] = jnp.zeros_like(n_ref)
        m_ref[...] = jnp.full_like(m_ref, M_INIT)
        bm = _lane_cumsum(lfm_ref[...])

        def load_meta(h):
            return (qm_ref[:, h * dk:(h + 1) * dk], km_ref[:, h * dk:(h + 1) * dk],
                    vm_ref[:, h * dv:(h + 1) * dv], om_ref[:, h * dv:(h + 1) * dv],
                    lim_ref[h:h + 1, 0:N_META], bm[h:h + 1, 0:N_META], norm_ref[:, h * dv:(h + 1) * dv])

        def store_meta(h, y):
            outm_ref[:, h * dv:(h + 1) * dv] = y.astype(outm_ref.dtype)

        _mlstm_heads(load_meta, store_meta, c_ref, n_ref, m_ref, heads, N_META)

    def body(c, carry):
        r0 = pl.multiple_of(c * CHUNK, CHUNK)
        rows = pl.ds(r0, CHUNK)
        gc = g * chunks + c

        def load(h):
            return (q_ref[rows, h * dk:(h + 1) * dk], k_ref[rows, h * dk:(h + 1) * dk],
                    v_ref[rows, h * dv:(h + 1) * dv], o_ref[rows, h * dv:(h + 1) * dv],
                    li_ref[h, pl.ds(gc, 1), :][:, 0:CHUNK], _lane_cumsum(lf_ref[h, pl.ds(gc, 1), :])[:, 0:CHUNK],
                    norm_ref[:, h * dv:(h + 1) * dv])

        def store(h, y):
            out_ref[rows, h * dv:(h + 1) * dv] = y.astype(out_ref.dtype)

        _mlstm_heads(load, store, c_ref, n_ref, m_ref, heads, CHUNK)
        return carry

    lax.fori_loop(0, chunks, body, 0)


def _mlstm_prompt(z, li, lf, li_meta, lf_meta, norm, seq, heads, dk, dv, mix_width):
    chunks = 4
    rb = chunks * CHUNK
    qw, vw = heads * dk, heads * dv
    meta_blk = seq // N_META
    spec = lambda w, c: pl.BlockSpec((rb, w), lambda g, c=c: (g, c))
    mspec = lambda w, c: pl.BlockSpec((N_META, w), lambda g, c=c: (meta_blk, c))
    gspec = pl.BlockSpec((heads, seq // CHUNK, LANES), lambda g: (0, 0, 0))
    const2 = lambda shape: pl.BlockSpec(shape, lambda g: (0,) * len(shape))
    return pl.pallas_call(
        functools.partial(_mlstm_prompt_kernel, heads=heads, dk=dk, dv=dv, chunks=chunks),
        grid=(seq // rb,),
        in_specs=[spec(qw, 0), spec(qw, 1), spec(vw, 1), spec(vw, 2), gspec, gspec,
                  mspec(qw, 0), mspec(qw, 1), mspec(vw, 1), mspec(vw, 2),
                  const2((heads, LANES)), const2((heads, LANES)), const2((1, vw))],
        out_specs=[pl.BlockSpec((rb, vw), lambda g: (g, 0)), const2((N_META, vw)),
                   const2((heads, dk, dv)), const2((heads, dk)), const2((heads, LANES))],
        out_shape=[jax.ShapeDtypeStruct((z.shape[0], mix_width), BF16), jax.ShapeDtypeStruct((N_META, vw), BF16),
                   jax.ShapeDtypeStruct((heads, dk, dv), F32), jax.ShapeDtypeStruct((heads, dk), F32),
                   jax.ShapeDtypeStruct((heads, LANES), F32)],
        compiler_params=_params("arbitrary"),
        name="mlstm_prompt",
    )(z, z, z, z, li, lf, z, z, z, z, li_meta, lf_meta, norm)


def _mlstm_sample_kernel(q_ref, k_ref, v_ref, o_ref, li_ref, lf_ref, c0_ref, n0_ref, m0_ref, norm_ref,
                         out_ref, c_ref, n_ref, m_ref, *, heads, dk, dv, valid):
    c_ref[...] = c0_ref[...]
    n_ref[...] = n0_ref[...]
    m_ref[...] = m0_ref[...]
    t = q_ref.shape[0]
    b_all = _lane_cumsum(lf_ref[...])

    def load(h):
        return (q_ref[:, h * dk:(h + 1) * dk], k_ref[:, h * dk:(h + 1) * dk],
                v_ref[:, h * dv:(h + 1) * dv], o_ref[:, h * dv:(h + 1) * dv],
                li_ref[h:h + 1, 0:t], b_all[h:h + 1, 0:t], norm_ref[:, h * dv:(h + 1) * dv])

    def store(h, y):
        out_ref[:, h * dv:(h + 1) * dv] = y

    _mlstm_heads(load, store, c_ref, n_ref, m_ref, heads, valid)


def _mlstm_sample(zs, li, lf, c0, n0, m0, norm, heads, dk, dv, valid):
    nb, t, _ = zs.shape
    qw, vw = heads * dk, heads * dv
    spec = lambda w, c: pl.BlockSpec((None, t, w), lambda b, c=c: (b, 0, c))
    per_b = lambda *shape: pl.BlockSpec((None,) + shape, lambda b: (b,) + (0,) * len(shape))
    return pl.pallas_call(
        functools.partial(_mlstm_sample_kernel, heads=heads, dk=dk, dv=dv, valid=valid),
        grid=(nb,),
        in_specs=[spec(qw, 0), spec(qw, 1), spec(vw, 1), spec(vw, 2), per_b(heads, LANES), per_b(heads, LANES),
                  per_b(heads, dk, dv), per_b(heads, dk), per_b(heads, LANES),
                  pl.BlockSpec((1, vw), lambda b: (0, 0))],
        out_specs=[per_b(t, vw), per_b(heads, dk, dv), per_b(heads, dk), per_b(heads, LANES)],
        out_shape=[jax.ShapeDtypeStruct((nb, t, vw), F32), jax.ShapeDtypeStruct((nb, heads, dk, dv), F32),
                   jax.ShapeDtypeStruct((nb, heads, dk), F32), jax.ShapeDtypeStruct((nb, heads, LANES), F32)],
        compiler_params=_params("parallel"),
        name="mlstm_sample",
    )(zs, zs, zs, zs, li, lf, c0, n0, m0, norm)


def _lambda_value(lp_ref, lam_init):
    lp = lp_ref[...]
    a = jnp.sum(lp[0:1, :] * lp[1:2, :], axis=1, keepdims=True)
    b = jnp.sum(lp[2:3, :] * lp[3:4, :], axis=1, keepdims=True)
    return jnp.exp(a) - jnp.exp(b) + lam_init


def _scores_t(q, kb, mask):
    s = _dot_nt(kb, q)
    return s if mask is None else jnp.where(mask, s, MASK_VALUE)


def _softmax_step_t(q, kb, vb, m_ref, l_ref, acc_ref, mask):
    _softmax_update_t(_scores_t(q, kb, mask), vb, m_ref, l_ref, acc_ref)


def _softmax_update_t(s, vb, m_ref, l_ref, acc_ref, cols=slice(None)):
    m_old = m_ref[:, cols]
    m_new = jnp.maximum(m_old, jnp.max(s, axis=0, keepdims=True))
    alpha = jnp.exp2(m_old - m_new)
    p = jnp.exp2(s - m_new)
    l_ref[:, cols] = alpha * l_ref[:, cols] + jnp.sum(p, axis=0, keepdims=True)
    acc_ref[:, cols] = alpha * acc_ref[:, cols] + _dot_tn(vb, p)
    m_ref[:, cols] = m_new


def _diff_finish_t(lam, sub_ref, l0_ref, a0_ref, l1_ref, a1_ref):
    o_t = a0_ref[...] / l0_ref[...] - lam * (a1_ref[...] / l1_ref[...])
    o = o_t.T
    return o * lax.rsqrt(jnp.mean(o * o, axis=-1, keepdims=True) + EPS) * sub_ref[...]


def _init_softmax_state(m_ref, l_ref, acc_ref):
    m_ref[...] = jnp.full_like(m_ref, MASK_VALUE)
    l_ref[...] = jnp.zeros_like(l_ref)
    acc_ref[...] = jnp.zeros_like(acc_ref)


def _flash_kernel(q0_ref, q1_ref, k_ref, v_ref, lp_ref, sub_ref, mix_ref, out_ref, *state, seq, tq, tk, lam_init):
    del mix_ref
    qi = pl.program_id(1)
    groups = tq // tk
    chains = [state[6 * g:6 * g + 6] for g in range(groups)]
    km, vm = k_ref[seq:seq + N_META, :], v_ref[seq:seq + N_META, :]
    queries = []
    for g, (m0, l0, a0, m1, l1, a1) in enumerate(chains):
        q0 = q0_ref[g * tk:(g + 1) * tk, :]
        q1 = q1_ref[g * tk:(g + 1) * tk, :]
        queries.append((q0, q1))
        _init_softmax_state(m0, l0, a0)
        _init_softmax_state(m1, l1, a1)
        _softmax_step_t(q0, km, vm, m0, l0, a0, None)
        _softmax_step_t(q1, km, vm, m1, l1, a1, None)

    ki = lax.broadcasted_iota(jnp.int32, (tk, tk), 0)
    qj = lax.broadcasted_iota(jnp.int32, (tk, tk), 1)
    causal = ki <= qj

    def step(j, first_group, diagonal):
        rows = pl.ds(pl.multiple_of(j * tk, tk), tk)
        kb, vb = k_ref[rows, :], v_ref[rows, :]
        scores = []
        for g in range(first_group, groups):
            mask = causal if (diagonal and g == first_group) else None
            scores.append((_scores_t(queries[g][0], kb, mask), _scores_t(queries[g][1], kb, mask)))
        for g, (s0, s1) in zip(range(first_group, groups), scores):
            m0, l0, a0, m1, l1, a1 = chains[g]
            _softmax_update_t(s0, vb, m0, l0, a0)
            _softmax_update_t(s1, vb, m1, l1, a1)

    def body(j, carry):
        step(j, 0, False)
        return carry

    lax.fori_loop(0, qi * groups, body, 0)
    for g in range(groups):
        step(qi * groups + g, g, True)

    lam = _lambda_value(lp_ref, lam_init)
    for g, (m0, l0, a0, m1, l1, a1) in enumerate(chains):
        out_ref[g * tk:(g + 1) * tk, :] = _diff_finish_t(lam, sub_ref, l0, a0, l1, a1).astype(out_ref.dtype)


def _flash(q0, q1, kb, vb, lam_p, subln, mix, seq, lam_init):
    width = q0.shape[1]
    heads = width // DIFF_DV
    col0 = (mix.shape[1] - width) // DIFF_DV
    tq = min(FLASH_Q_TILE, seq)
    tk = min(FLASH_K_TILE, tq)
    kv_rows = seq + N_META
    qspec = pl.BlockSpec((tq, DIFF_DV), lambda h, i: (i, h))
    kspec = pl.BlockSpec((kv_rows, DIFF_DV), lambda h, i: (0, h))
    stat = pltpu.VMEM((1, tk), F32)
    acc = pltpu.VMEM((DIFF_DV, tk), F32)
    return pl.pallas_call(
        functools.partial(_flash_kernel, seq=seq, tq=tq, tk=tk, lam_init=lam_init),
        grid=(heads, seq // tq),
        in_specs=[qspec, qspec, kspec, kspec,
                  pl.BlockSpec(lam_p.shape, lambda h, i: (0, 0)), pl.BlockSpec((1, DIFF_DV), lambda h, i: (0, 0)),
                  pl.BlockSpec(memory_space=pl.ANY)],
        out_specs=pl.BlockSpec((tq, DIFF_DV), lambda h, i: (i, col0 + h)),
        out_shape=jax.ShapeDtypeStruct(mix.shape, mix.dtype),
        input_output_aliases={6: 0},
        scratch_shapes=[stat, stat, acc, stat, stat, acc] * (tq // tk),
        compiler_params=_params("parallel", "arbitrary"),
        name="diff_attn_prompt",
    )(q0, q1, kb, vb, lam_p, subln, mix)


def _meta_attn_kernel(q0_ref, q1_ref, k_ref, v_ref, lp_ref, sub_ref, out_ref,
                      m0, l0, a0, m1, l1, a1, *, lam_init):
    _init_softmax_state(m0, l0, a0)
    _init_softmax_state(m1, l1, a1)
    t = q0_ref.shape[0]
    ki = lax.broadcasted_iota(jnp.int32, (t, t), 0)
    qj = lax.broadcasted_iota(jnp.int32, (t, t), 1)
    mask = ki <= qj
    _softmax_step_t(q0_ref[...], k_ref[...], v_ref[...], m0, l0, a0, mask)
    _softmax_step_t(q1_ref[...], k_ref[...], v_ref[...], m1, l1, a1, mask)
    out_ref[...] = _diff_finish_t(_lambda_value(lp_ref, lam_init), sub_ref, l0, a0, l1, a1)


def _meta_attn(q0, q1, kb, vb, lam_p, subln, lam_init):
    t, width = q0.shape
    heads = width // DIFF_DV
    spec = pl.BlockSpec((t, DIFF_DV), lambda h: (0, h))
    stat = pltpu.VMEM((1, t), F32)
    acc = pltpu.VMEM((DIFF_DV, t), F32)
    return pl.pallas_call(
        functools.partial(_meta_attn_kernel, lam_init=lam_init),
        grid=(heads,),
        in_specs=[spec, spec, spec, spec,
                  pl.BlockSpec(lam_p.shape, lambda h: (0, 0)), pl.BlockSpec((1, DIFF_DV), lambda h: (0, 0))],
        out_specs=spec,
        out_shape=jax.ShapeDtypeStruct((t, width), F32),
        scratch_shapes=[stat, stat, acc, stat, stat, acc],
        compiler_params=_params("parallel"),
        name="diff_attn_meta",
    )(q0, q1, kb, vb, lam_p, subln)


def _sample_attn_kernel(pt_ref, q_ref, *refs, pages, heads, new_valid, q_len, lam_init):
    k_refs = refs[:pages]
    v_refs = refs[pages:2 * pages]
    kn_ref, vn_ref, lp_ref, sub_ref, out_ref, kcat, vcat, m_s, l_s, acc_s = refs[2 * pages:]
    j = pl.program_id(1)
    page = k_refs[0].shape[1]
    keys = PAGES_PER_UPDATE * page
    rows_q = q_ref.shape[0]
    group = rows_q // heads

    @pl.when(j == 0)
    def _():
        m_s[...] = jnp.full_like(m_s, MASK_VALUE)
        l_s[...] = jnp.zeros_like(l_s)
        acc_s[...] = jnp.zeros_like(acc_s)

    q = q_ref[...]

    def update(k_t, weighted_values, mask):
        s = jnp.dot(q, k_t, preferred_element_type=F32)
        if mask is not None:
            s = jnp.where(mask, s, MASK_VALUE)
        m_old = m_s[...]
        m_new = jnp.maximum(m_old, jnp.max(s, axis=1, keepdims=True))
        alpha = jnp.exp2(m_old - m_new)
        p = jnp.exp2(s - m_new)
        l_s[...] = alpha * l_s[...] + jnp.sum(p, axis=1, keepdims=True)
        acc_s[...] = alpha * acc_s[...] + weighted_values(p)
        m_s[...] = m_new

    def cache_values(p):
        own_head = (lax.broadcasted_iota(jnp.int32, (heads, rows_q), 0)
                    == lax.broadcasted_iota(jnp.int32, (heads, rows_q), 1) // group)
        p_t = p.T
        expanded = jnp.where(own_head[None], p_t[:, None, :], 0.0).reshape(keys * heads, rows_q)
        return _dot_tn(expanded, vcat[...])

    def new_values(p):
        return jnp.concatenate(
            [jnp.dot(p[h * group:(h + 1) * group, :].astype(BF16), vn_ref[:, h * DIFF_DV:(h + 1) * DIFF_DV],
                     preferred_element_type=F32) for h in range(heads)], axis=0)

    for first in range(0, pages, PAGES_PER_UPDATE):
        for i in range(PAGES_PER_UPDATE):
            kcat[:, i * page:(i + 1) * page] = k_refs[first + i][...].astype(BF16)
            vcat[i * page * heads:(i + 1) * page * heads, :] = v_refs[first + i][...].astype(BF16)
        update(kcat[...], cache_values, None)

    @pl.when(j == pl.num_programs(1) - 1)
    def _():
        n_new = kn_ref.shape[1]
        r = lax.broadcasted_iota(jnp.int32, (rows_q, n_new), 0)
        u = lax.broadcasted_iota(jnp.int32, (rows_q, n_new), 1)
        update(kn_ref[...], new_values, (u <= r % q_len) & (u < new_valid))
        on = acc_s[...] / l_s[...]
        lam = _lambda_value(lp_ref, lam_init)
        d = on - lam * pltpu.roll(on, rows_q - q_len, 0)
        out_ref[...] = d * lax.rsqrt(jnp.mean(d * d, axis=-1, keepdims=True) + EPS) * sub_ref[...]


def _sample_attn(page_table, qbd, cache_kt, cache_v, k_new_t, v_new, lam_p, subln, q_len, lam_init):
    nb, rows_q, width = qbd.shape
    heads = width // DIFF_DV
    page = cache_kt.shape[2]
    n_pages = page_table.shape[1]
    pages = PAGES_PER_STEP
    k_spec = lambda i: pl.BlockSpec((None, width, page), lambda b, j, pt, i=i: (pt[b, j * pages + i], 0, 0))
    v_spec = lambda i: pl.BlockSpec((None, page * heads, DIFF_DV),
                                    lambda b, j, pt, i=i: (pt[b, j * pages + i], 0, 0))
    per_b = lambda shape: pl.BlockSpec((None,) + shape, lambda b, j, pt: (b, 0, 0))
    grid_spec = pltpu.PrefetchScalarGridSpec(
        num_scalar_prefetch=1,
        grid=(nb, n_pages // pages),
        in_specs=[per_b((rows_q, width))] + [k_spec(i) for i in range(pages)] + [v_spec(i) for i in range(pages)]
                 + [per_b(k_new_t.shape[1:]), per_b(v_new.shape[1:]),
                    pl.BlockSpec(lam_p.shape, lambda b, j, pt: (0, 0)),
                    pl.BlockSpec((1, DIFF_DV), lambda b, j, pt: (0, 0))],
        out_specs=pl.BlockSpec((None, rows_q, DIFF_DV), lambda b, j, pt: (b, 0, 0)),
        scratch_shapes=[pltpu.VMEM((width, PAGES_PER_UPDATE * page), BF16),
                        pltpu.VMEM((heads * PAGES_PER_UPDATE * page, DIFF_DV), BF16),
                        pltpu.VMEM((rows_q, 1), F32), pltpu.VMEM((rows_q, 1), F32),
                        pltpu.VMEM((rows_q, DIFF_DV), F32)],
    )
    return pl.pallas_call(
        functools.partial(_sample_attn_kernel, pages=pages, heads=heads, new_valid=q_len, q_len=q_len,
                          lam_init=lam_init),
        grid_spec=grid_spec,
        out_shape=jax.ShapeDtypeStruct((nb, rows_q, DIFF_DV), F32),
        compiler_params=_params("parallel", "arbitrary"),
        name="diff_attn_sample",
    )(page_table, qbd, *([cache_kt] * pages), *([cache_v] * pages), k_new_t, v_new, lam_p, subln)


def _rope_tables(pos):
    half = DIFF_DH // 2
    inv = ROPE_THETA ** (-jnp.arange(half, dtype=F32) / half)
    ang = jnp.asarray(pos, F32)[:, None] * inv[None, :]
    cos, sin = jnp.cos(ang), jnp.sin(ang)
    cos = jnp.tile(cos, (1, LANES // half))
    sin = jnp.tile(jnp.concatenate([-sin, sin], axis=1), (1, LANES // DIFF_DH))
    return cos, sin


def kernel(x_prompt, x_sample, cache_k, cache_v, state_C, state_n, state_m, page_table, meta_tokens,
           norm_ffn1, ffn1_w_gate, ffn1_w_up, ffn1_w_down, norm_mix, w_in, mlstm_b_i, mlstm_b_f,
           mlstm_norm, diff_lambda_qk, diff_subln, w_out, norm_ffn2, ffn2_w_gate, ffn2_w_up,
           ffn2_w_down, norm_final):
    batch, seq, d_model = x_prompt.shape
    assert batch == 1 and state_C.shape[0] == 1
    dec_b, dec_t, _ = x_sample.shape
    heads_m = mlstm_b_i.shape[1]
    dk, dv = state_C.shape[3], state_C.shape[4]
    qw, vw = heads_m * dk, heads_m * dv
    dw = w_out.shape[1] - vw
    heads_d = dw // DIFF_DV
    assert vw == 2 * qw and dw == vw, "column blocks of the projection are addressed in units of the q width"
    n_s = dec_b * dec_t
    rows = seq + N_META + n_s
    past = page_table.shape[1] * cache_k.shape[2]
    lam_init = 0.8 - 0.6 * math.exp(-0.0)

    def ffn_weights(wg, wu, wd):
        return wg, wu, wd.astype(BF16)

    h = jnp.concatenate([x_prompt[0], meta_tokens.astype(F32), x_sample.reshape(n_s, d_model)], axis=0)

    h = _ffn(h, norm_ffn1[0], *ffn_weights(ffn1_w_gate[0], ffn1_w_up[0], ffn1_w_down[0]))

    wt = w_in[0].T
    c_gate = 2 * qw + 2 * vw
    wt_gate = jnp.pad(wt[c_gate:c_gate + 2 * heads_m], ((0, LANES - 2 * heads_m), (0, 0)))
    u = _rmsnorm(h, norm_mix[0], BF16, 256)
    tm = _row_tile(rows, 1088)
    z = _mm_nt(u, wt, tm, _col_tile(c_gate), c_gate)
    zg = _mm_nt(u, wt_gate, tm, LANES)
    pos = np.concatenate([N_META + np.arange(seq), np.arange(N_META), past + np.tile(np.arange(dec_t), dec_b)])
    cos, sin = _rope_tables(pos)
    wt_d = wt[c_gate + 2 * heads_m:].astype(BF16)
    tn_d = _col_tile(dw)
    q0, q1, k_rot, kb, v_all, vb = _proj_rope(u, wt_d, dw, cos, sin, tm, tn_d)
    gate_bias = jnp.pad(jnp.concatenate([mlstm_b_i[0], mlstm_b_f[0]]), (0, LANES - 2 * heads_m)).reshape(1, LANES)
    gates = _gates(zg, gate_bias.astype(F32), heads_m)
    log_i, log_f = gates[:, :heads_m], gates[:, heads_m:2 * heads_m]

    def chunk_rows(a):
        a = a.T.reshape(heads_m, seq // CHUNK, CHUNK)
        return jnp.pad(a, ((0, 0), (0, 0), (0, LANES - CHUNK)))

    lane_rows = lambda a: jnp.pad(a, ((0, 0),) * (a.ndim - 1) + ((0, LANES - a.shape[-1]),))
    norm_m = mlstm_norm[0].reshape(1, vw).astype(F32)
    mix, mo_meta, c_p, n_p, m_p = _mlstm_prompt(
        z, chunk_rows(log_i[:seq]), chunk_rows(log_f[:seq]),
        lane_rows(log_i[seq:seq + N_META].T), lane_rows(log_f[seq:seq + N_META].T), norm_m, seq, heads_m, dk, dv,
        vw + dw)

    t_pad = 8
    zs = jnp.pad(z[seq + N_META:].reshape(dec_b, dec_t, c_gate), ((0, 0), (0, t_pad - dec_t), (0, 0)))
    gate_s = lambda a: lane_rows(a[seq + N_META:].reshape(dec_b, dec_t, heads_m).transpose(0, 2, 1))
    m0 = jnp.broadcast_to(state_m[0][:, :, None], (dec_b, heads_m, LANES))
    mo_s, c_s, n_s_new, m_s = _mlstm_sample(zs, gate_s(log_i), gate_s(log_f), state_C[0], state_n[0], m0, norm_m,
                                             heads_m, dk, dv, dec_t)

    sub =(diff_subln[0].astype(F32) * (1.0 - lam_init)).reshape(1, DIFF_DV)
    lam_p = diff_lambda_qk[0].astype(F32)
    mix = _flash(q0, q1, kb, vb, lam_p, sub, mix, seq, lam_init)

    pad128 = lambda a: jnp.pad(a, ((0, LANES - a.shape[0]), (0, 0)))
    meta = slice(seq, seq + N_META)
    do_meta = _meta_attn(pad128(q0[meta]), pad128(q1[meta]), pad128(kb[meta]), pad128(vb[meta]), lam_p, sub,
                         lam_init)[:N_META]

    qs = (q0[seq + N_META:] + q1[seq + N_META:]).reshape(dec_b, 1, dec_t, dw)
    rows_q = heads_d * 2 * dec_t
    qbd = jnp.broadcast_to(qs, (dec_b, heads_d * 2, dec_t, dw)).reshape(dec_b, rows_q, dw)
    r_id = np.arange(rows_q)[:, None] // dec_t
    c_id = np.arange(dw)[None, :] // DIFF_DH
    qbd = jnp.where(jnp.asarray(r_id == c_id), qbd, jnp.zeros_like(qbd))
    new_rows = lambda a: jnp.pad(a[seq + N_META:].reshape(dec_b, dec_t, dw), ((0, 0), (0, LANES - dec_t), (0, 0)))
    n_phys, page = cache_k.shape[1], cache_k.shape[2]
    cache_kt = jnp.transpose(cache_k[0], (0, 2, 3, 4, 1)).reshape(n_phys, dw, page)
    do_s = _sample_attn(page_table, qbd, cache_kt, cache_v[0].reshape(n_phys, page * heads_d, DIFF_DV),
                        new_rows(kb).transpose(0, 2, 1), new_rows(vb), lam_p, sub, dec_t, lam_init)
    do_s = do_s.reshape(dec_b, heads_d, 2, dec_t, DIFF_DV)[:, :, 0].transpose(0, 2, 1, 3).reshape(n_s, dw)

    tail = jnp.concatenate([
        jnp.concatenate([mo_meta, mo_s[:, :dec_t].reshape(n_s, vw).astype(BF16)], axis=0),
        jnp.concatenate([do_meta.astype(BF16), do_s.astype(BF16)], axis=0)], axis=1)
    mix = lax.dynamic_update_slice(mix, tail, (seq, 0))
    h = _mm_resid(mix, w_out[0], h, 1.0, _row_tile(rows, 1088), 512)

    h = _ffn(h, norm_ffn2[0], *ffn_weights(ffn2_w_gate[0], ffn2_w_up[0], ffn2_w_down[0]))

    y_prompt = _rmsnorm(h, norm_final, F32, 256, 0, seq // 256).reshape(1, seq, d_model)
    y_sample = _rmsnorm(h, norm_final, F32, N_META, (seq + N_META) // N_META, n_s // N_META)
    y_sample = y_sample.reshape(dec_b, dec_t, d_model)

    order = lambda a: jnp.concatenate([a[meta], a[:seq]], axis=0)
    k_prompt = order(k_rot).reshape(1, 1, seq + N_META, heads_d, 2, DIFF_DH)
    v_prompt = order(v_all).reshape(1, 1, seq + N_META, heads_d, DIFF_DV)
    k_sample = k_rot[seq + N_META:].reshape(1, dec_b, dec_t, heads_d, 2, DIFF_DH)
    v_sample = v_all[seq + N_META:].reshape(1, dec_b, dec_t, heads_d, DIFF_DV)
    return (y_prompt, y_sample, k_prompt, v_prompt,
            c_p[None, None], n_p[None, None], m_p[None, None, :, 0],
            k_sample, v_sample, c_s[None], n_s_new[None], m_s[None, :, :, 0])
```

```python
import functools
import math

import numpy as np
import jax
import jax.numpy as jnp
from jax import lax
from jax.experimental import pallas as pl
from jax.experimental.pallas import tpu as pltpu

F32 = jnp.float32
BF16 = jnp.bfloat16

N_META = 16
CHUNK = 64
ROPE_THETA = 10000.0
GATE_CAP = 15.0
M_INIT = -1e30
EPS = 1e-6
DIFF_DV = 128
DIFF_DH = DIFF_DV // 2
DIFF_SCALE = DIFF_DH ** -0.5
Q_SCALE_LOG2 = DIFF_SCALE * math.log2(math.e)
PAGES_PER_STEP = 8
PAGES_PER_UPDATE = 4
FLASH_Q_TILE = 2048
FLASH_K_TILE = 512
MASK_VALUE = -1e30

LANES = 128
V7X_VMEM_BYTES = 64 * 2 ** 20
VMEM_LIMIT = V7X_VMEM_BYTES - 8 * 2 ** 20


def _params(*sem):
    return pltpu.CompilerParams(dimension_semantics=sem, vmem_limit_bytes=VMEM_LIMIT)


def _round_up(x, m):
    return (x + m - 1) // m * m


def _row_tile(rows, target):
    n = pl.cdiv(rows, target)
    return _round_up(pl.cdiv(rows, n), 16)


def _col_tile(n):
    return 512 if n % 512 == 0 else 256


def _dot(a, b):
    return jnp.dot(a.astype(BF16), b.astype(BF16), preferred_element_type=F32)


def _dot_nt(a, b):
    return lax.dot_general(a.astype(BF16), b.astype(BF16), (((1,), (1,)), ((), ())),
                           preferred_element_type=F32)


def _dot_tn(a, b):
    return lax.dot_general(a.astype(BF16), b.astype(BF16), (((0,), (0,)), ((), ())),
                           preferred_element_type=F32)


def _rmsnorm_kernel(x_ref, g_ref, o_ref):
    x = x_ref[...]
    y = x * lax.rsqrt(jnp.mean(x * x, axis=-1, keepdims=True) + EPS)
    o_ref[...] = (y * g_ref[...]).astype(o_ref.dtype)


def _rmsnorm(x, g, out_dtype, row_block, first_block=0, n_blocks=None):
    rows, d = x.shape
    if n_blocks is None:
        n_blocks, out_rows = pl.cdiv(rows, row_block), rows
    else:
        out_rows = n_blocks * row_block
    return pl.pallas_call(
        _rmsnorm_kernel,
        grid=(n_blocks,),
        in_specs=[pl.BlockSpec((row_block, d), lambda i: (i + first_block, 0)),
                  pl.BlockSpec((1, d), lambda i: (0, 0))],
        out_specs=pl.BlockSpec((row_block, d), lambda i: (i, 0)),
        out_shape=jax.ShapeDtypeStruct((out_rows, d), out_dtype),
        compiler_params=_params("parallel"),
        name="rmsnorm",
    )(x, g.reshape(1, d).astype(F32))


def _gateup_kernel(x_ref, wg_ref, wu_ref, o_ref):
    x = x_ref[...]
    g = jnp.dot(x, wg_ref[...].astype(BF16), preferred_element_type=F32)
    u = jnp.dot(x, wu_ref[...].astype(BF16), preferred_element_type=F32)
    o_ref[...] = (g * jax.nn.sigmoid(g) * u).astype(o_ref.dtype)


def _gateup(x, wg, wu, tm, tn):
    rows, d = x.shape
    n = wg.shape[1]
    return pl.pallas_call(
        _gateup_kernel,
        grid=(pl.cdiv(rows, tm), n // tn),
        in_specs=[pl.BlockSpec((tm, d), lambda i, j: (i, 0), pipeline_mode=pl.Buffered(1)),
                  pl.BlockSpec((d, tn), lambda i, j: (0, j)),
                  pl.BlockSpec((d, tn), lambda i, j: (0, j))],
        out_specs=pl.BlockSpec((tm, tn), lambda i, j: (i, j)),
        out_shape=jax.ShapeDtypeStruct((rows, n), BF16),
        compiler_params=_params("parallel", "arbitrary"),
        name="ffn_gate_up",
    )(x, wg, wu)


def _mm_resid_kernel(a_ref, w_ref, r_ref, o_ref, *, scale):
    acc = jnp.dot(a_ref[...], w_ref[...].astype(BF16), preferred_element_type=F32)
    o_ref[...] = r_ref[...] + scale * acc


def _mm_resid(a, w, resid, scale, tm, tn):
    rows, k = a.shape
    n = w.shape[1]
    return pl.pallas_call(
        functools.partial(_mm_resid_kernel, scale=scale),
        grid=(pl.cdiv(rows, tm), n // tn),
        in_specs=[pl.BlockSpec((tm, k), lambda i, j: (i, 0)),
                  pl.BlockSpec((k, tn), lambda i, j: (0, j)),
                  pl.BlockSpec((tm, tn), lambda i, j: (i, j))],
        out_specs=pl.BlockSpec((tm, tn), lambda i, j: (i, j)),
        out_shape=jax.ShapeDtypeStruct((rows, n), F32),
        compiler_params=_params("parallel", "arbitrary"),
        name="matmul_residual",
    )(a, w, resid)


def _mm_nt_kernel(a_ref, wt_ref, o_ref):
    o_ref[...] = _dot_nt(a_ref[...], wt_ref[...])


def _mm_nt(a, wt, tm, tn, n=None):
    rows, k = a.shape
    n = wt.shape[0] if n is None else n
    return pl.pallas_call(
        _mm_nt_kernel,
        grid=(pl.cdiv(rows, tm), n // tn),
        in_specs=[pl.BlockSpec((tm, k), lambda i, j: (i, 0), pipeline_mode=pl.Buffered(1)),
                  pl.BlockSpec((tn, k), lambda i, j: (j, 0))],
        out_specs=pl.BlockSpec((tm, tn), lambda i, j: (i, j)),
        out_shape=jax.ShapeDtypeStruct((rows, n), F32),
        compiler_params=_params("parallel", "arbitrary"),
        name="matmul_nt",
    )(a, wt)


def _ffn(h, g, wg, wu, wd):
    rows = h.shape[0]
    xn = _rmsnorm(h, g, BF16, 256)
    hid = _gateup(xn, wg, wu, _row_tile(rows, 2176), _col_tile(wg.shape[1]))
    return _mm_resid(hid, wd, h, 0.5, _row_tile(rows, 544), 512)


def _gate_kernel(z_ref, b_ref, o_ref, *, heads):
    z = z_ref[...] + b_ref[...]
    c = GATE_CAP * jnp.tanh(z / GATE_CAP)
    log_sig = jnp.minimum(c, 0.0) - jnp.log1p(jnp.exp(-jnp.abs(c)))
    lane = lax.broadcasted_iota(jnp.int32, c.shape, 1)
    o_ref[...] = jnp.where(lane < heads, c, log_sig)


def _gates(z, bias, heads):
    rows, d = z.shape
    rb = 512
    return pl.pallas_call(
        functools.partial(_gate_kernel, heads=heads),
        grid=(pl.cdiv(rows, rb),),
        in_specs=[pl.BlockSpec((rb, d), lambda i: (i, 0)), pl.BlockSpec((1, d), lambda i: (0, 0))],
        out_specs=pl.BlockSpec((rb, d), lambda i: (i, 0)),
        out_shape=jax.ShapeDtypeStruct((rows, d), F32),
        compiler_params=_params("parallel"),
        name="mlstm_gates",
    )(z, bias)


def _proj_rope_kernel(a_ref, wt_ref, cos_ref, sin_ref, o0_ref, o1_ref, *, mode):
    z = _dot_nt(a_ref[...], wt_ref[...])
    if mode == "v":
        o0_ref[...] = z
        o1_ref[...] = z.astype(BF16)
        return
    cos = cos_ref[...]
    sin = sin_ref[...]
    lane = lax.broadcasted_iota(jnp.int32, cos.shape, 1)
    first_half = (lane % DIFF_DH) < (DIFF_DH // 2)
    map0 = lane < DIFF_DH
    for j in range(z.shape[1] // LANES):
        sl = slice(j * LANES, (j + 1) * LANES)
        x = z[:, sl]
        partner = jnp.where(first_half, pltpu.roll(x, LANES - DIFF_DH // 2, 1), pltpu.roll(x, DIFF_DH // 2, 1))
        r = x * cos + partner * sin
        if mode == "q":
            r = r * Q_SCALE_LOG2
            o0_ref[:, sl] = jnp.where(map0, r, 0.0).astype(BF16)
            o1_ref[:, sl] = jnp.where(map0, 0.0, r).astype(BF16)
        else:
            o0_ref[:, sl] = r
            o1_ref[:, sl] = r.astype(BF16)


def _proj_rope(a, wt, row0, width, cos, sin, mode, tm, tn):
    rows, k = a.shape
    off = row0 // tn
    tspec = pl.BlockSpec((tm, LANES), lambda i, j: (i, 0))
    ospec = pl.BlockSpec((tm, tn), lambda i, j: (i, j))
    return pl.pallas_call(
        functools.partial(_proj_rope_kernel, mode=mode),
        grid=(pl.cdiv(rows, tm), width // tn),
        in_specs=[pl.BlockSpec((tm, k), lambda i, j: (i, 0)),
                  pl.BlockSpec((tn, k), lambda i, j: (j + off, 0)), tspec, tspec],
        out_specs=[ospec, ospec],
        out_shape=[jax.ShapeDtypeStruct((rows, width), BF16 if mode == "q" else F32),
                   jax.ShapeDtypeStruct((rows, width), BF16)],
        compiler_params=_params("parallel", "arbitrary"),
        name="proj_rope_" + mode,
    )(a, wt, cos, sin)


def _lane_cumsum(x):
    lane = lax.broadcasted_iota(jnp.int32, x.shape, 1)
    k = 1
    while k < x.shape[1]:
        x = x + jnp.where(lane >= k, pltpu.roll(x, k, 1), 0.0)
        k *= 2
    return x


def _mlstm_heads(load, store, c_ref, n_ref, m_ref, heads, valid):
    ins = [load(h) for h in range(heads)]
    t, dk = ins[0][0].shape
    ri = lax.broadcasted_iota(jnp.int32, (t, t), 0)
    ci = lax.broadcasted_iota(jnp.int32, (t, t), 1)
    eye = ri == ci
    lane = lax.broadcasted_iota(jnp.int32, (1, t), 1)

    stage1 = []
    for h, (q, k, v, o, li_row, b_row, norm_row) in enumerate(ins):
        k = k * (dk ** -0.5)
        c_state = c_ref[h]
        stage1.append((k, c_state, _dot_nt(q, k), _dot(q, c_state)))

    stage2 = []
    for h, (q, k_raw, v, o, li_row, b_row, norm_row) in enumerate(ins):
        k, c_state, qk, qc = stage1[h]
        m_prev = m_ref[h:h + 1, 0:1]
        b_mat = jnp.broadcast_to(b_row, (t, t))
        b_col = jnp.sum(jnp.where(eye, b_mat, 0.0), axis=1, keepdims=True)
        log_d = jnp.where(ci <= ri, b_col - b_mat + li_row, -jnp.inf)
        m_inter = m_prev + b_col
        m_t = jnp.maximum(m_inter, jnp.max(log_d, axis=1, keepdims=True))
        w_inter = jnp.exp(m_inter - m_t)
        s = qk * jnp.exp(log_d - m_t)
        m_new = m_t[valid - 1:valid, :]
        decay = jnp.exp(m_inter[valid - 1:valid, :] - m_new)
        wk_row = jnp.where(lane < valid, jnp.exp(b_row[:, valid - 1:valid] - b_row + li_row - m_new), 0.0)
        wk_col = jnp.sum(jnp.where(eye, jnp.broadcast_to(wk_row, (t, t)), 0.0), axis=1, keepdims=True)
        stage2.append((s, k * wk_col, w_inter, m_t, m_new, decay))

    stage3 = []
    for h, (q, k_raw, v, o, li_row, b_row, norm_row) in enumerate(ins):
        s, kw = stage2[h][:2]
        stage3.append((_dot(s, v), _dot_tn(kw, v)))

    for h, (q, k_raw, v, o, li_row, b_row, norm_row) in enumerate(ins):
        k, c_state, qk, qc = stage1[h]
        s, kw, w_inter, m_t, m_new, decay = stage2[h]
        sv, kv = stage3[h]
        n_row = n_ref[h:h + 1, :]
        num = w_inter * qc + sv
        den = w_inter * jnp.sum(q * n_row, axis=1, keepdims=True) + jnp.sum(s, axis=1, keepdims=True)
        y = num / jnp.maximum(jnp.abs(den), jnp.exp(-m_t))
        c_ref[h] = decay * c_state + kv
        n_ref[h:h + 1, :] = decay * n_row + jnp.sum(kw, axis=0, keepdims=True)
        m_ref[h:h + 1, :] = jnp.broadcast_to(m_new, (1, m_ref.shape[1]))
        yn = y * lax.rsqrt(jnp.mean(y * y, axis=-1, keepdims=True) + EPS) * norm_row
        store(h, jax.nn.sigmoid(o) * yn)


def _mlstm_prompt_kernel(q_ref, k_ref, v_ref, o_ref, li_ref, lf_ref,
                         qm_ref, km_ref, vm_ref, om_ref, lim_ref, lfm_ref, norm_ref,
                         out_ref, outm_ref, c_ref, n_ref, m_ref, *, heads, dk, dv, chunks):
    g = pl.program_id(0)

    @pl.when(g == 0)
    def _():
        c_ref[...] = jnp.zeros_like(c_ref)
        n_ref[...] = jnp.zeros_like(n_ref)
        m_ref[...] = jnp.full_like(m_ref, M_INIT)
        bm = _lane_cumsum(lfm_ref[...])

        def load_meta(h):
            return (qm_ref[:, h * dk:(h + 1) * dk], km_ref[:, h * dk:(h + 1) * dk],
                    vm_ref[:, h * dv:(h + 1) * dv], om_ref[:, h * dv:(h + 1) * dv],
                    lim_ref[h:h + 1, 0:N_META], bm[h:h + 1, 0:N_META], norm_ref[:, h * dv:(h + 1) * dv])

        def store_meta(h, y):
            outm_ref[:, h * dv:(h + 1) * dv] = y.astype(outm_ref.dtype)

        _mlstm_heads(load_meta, store_meta, c_ref, n_ref, m_ref, heads, N_META)

    def body(c, carry):
        r0 = pl.multiple_of(c * CHUNK, CHUNK)
        rows = pl.ds(r0, CHUNK)
        gc = g * chunks + c

        def load(h):
            return (q_ref[rows, h * dk:(h + 1) * dk], k_ref[rows, h * dk:(h + 1) * dk],
                    v_ref[rows, h * dv:(h + 1) * dv], o_ref[rows, h * dv:(h + 1) * dv],
                    li_ref[h, pl.ds(gc, 1), :][:, 0:CHUNK], _lane_cumsum(lf_ref[h, pl.ds(gc, 1), :])[:, 0:CHUNK],
                    norm_ref[:, h * dv:(h + 1) * dv])

        def store(h, y):
            out_ref[rows, h * dv:(h + 1) * dv] = y.astype(out_ref.dtype)

        _mlstm_heads(load, store, c_ref, n_ref, m_ref, heads, CHUNK)
        return carry

    lax.fori_loop(0, chunks, body, 0)


def _mlstm_prompt(z, li, lf, li_meta, lf_meta, norm, seq, heads, dk, dv, mix_width):
    chunks = 4
    rb = chunks * CHUNK
    qw, vw = heads * dk, heads * dv
    meta_blk = seq // N_META
    spec = lambda w, c: pl.BlockSpec((rb, w), lambda g, c=c: (g, c))
    mspec = lambda w, c: pl.BlockSpec((N_META, w), lambda g, c=c: (meta_blk, c))
    gspec = pl.BlockSpec((heads, seq // CHUNK, LANES), lambda g: (0, 0, 0))
    const2 = lambda shape: pl.BlockSpec(shape, lambda g: (0,) * len(shape))
    return pl.pallas_call(
        functools.partial(_mlstm_prompt_kernel, heads=heads, dk=dk, dv=dv, chunks=chunks),
        grid=(seq // rb,),
        in_specs=[spec(qw, 0), spec(qw, 1), spec(vw, 1), spec(vw, 2), gspec, gspec,
                  mspec(qw, 0), mspec(qw, 1), mspec(vw, 1), mspec(vw, 2),
                  const2((heads, LANES)), const2((heads, LANES)), const2((1, vw))],
        out_specs=[pl.BlockSpec((rb, vw), lambda g: (g, 0)), const2((N_META, vw)),
                   const2((heads, dk, dv)), const2((heads, dk)), const2((heads, LANES))],
        out_shape=[jax.ShapeDtypeStruct((z.shape[0], mix_width), BF16), jax.ShapeDtypeStruct((N_META, vw), BF16),
                   jax.ShapeDtypeStruct((heads, dk, dv), F32), jax.ShapeDtypeStruct((heads, dk), F32),
                   jax.ShapeDtypeStruct((heads, LANES), F32)],
        compiler_params=_params("arbitrary"),
        name="mlstm_prompt",
    )(z, z, z, z, li, lf, z, z, z, z, li_meta, lf_meta, norm)


def _mlstm_sample_kernel(q_ref, k_ref, v_ref, o_ref, li_ref, lf_ref, c0_ref, n0_ref, m0_ref, norm_ref,
                         out_ref, c_ref, n_ref, m_ref, *, heads, dk, dv, valid):
    c_ref[...] = c0_ref[...]
    n_ref[...] = n0_ref[...]
    m_ref[...] = m0_ref[...]
    t = q_ref.shape[0]
    b_all = _lane_cumsum(lf_ref[...])

    def load(h):
        return (q_ref[:, h * dk:(h + 1) * dk], k_ref[:, h * dk:(h + 1) * dk],
                v_ref[:, h * dv:(h + 1) * dv], o_ref[:, h * dv:(h + 1) * dv],
                li_ref[h:h + 1, 0:t], b_all[h:h + 1, 0:t], norm_ref[:, h * dv:(h + 1) * dv])

    def store(h, y):
        out_ref[:, h * dv:(h + 1) * dv] = y

    _mlstm_heads(load, store, c_ref, n_ref, m_ref, heads, valid)


def _mlstm_sample(zs, li, lf, c0, n0, m0, norm, heads, dk, dv, valid):
    nb, t, _ = zs.shape
    qw, vw = heads * dk, heads * dv
    spec = lambda w, c: pl.BlockSpec((None, t, w), lambda b, c=c: (b, 0, c))
    per_b = lambda *shape: pl.BlockSpec((None,) + shape, lambda b: (b,) + (0,) * len(shape))
    return pl.pallas_call(
        functools.partial(_mlstm_sample_kernel, heads=heads, dk=dk, dv=dv, valid=valid),
        grid=(nb,),
        in_specs=[spec(qw, 0), spec(qw, 1), spec(vw, 1), spec(vw, 2), per_b(heads, LANES), per_b(heads, LANES),
                  per_b(heads, dk, dv), per_b(heads, dk), per_b(heads, LANES),
                  pl.BlockSpec((1, vw), lambda b: (0, 0))],
        out_specs=[per_b(t, vw), per_b(heads, dk, dv), per_b(heads, dk), per_b(heads, LANES)],
        out_shape=[jax.ShapeDtypeStruct((nb, t, vw), F32), jax.ShapeDtypeStruct((nb, heads, dk, dv), F32),
                   jax.ShapeDtypeStruct((nb, heads, dk), F32), jax.ShapeDtypeStruct((nb, heads, LANES), F32)],
        compiler_params=_params("parallel"),
        name="mlstm_sample",
    )(zs, zs, zs, zs, li, lf, c0, n0, m0, norm)


def _lambda_value(lp_ref, lam_init):
    lp = lp_ref[...]
    a = jnp.sum(lp[0:1, :] * lp[1:2, :], axis=1, keepdims=True)
    b = jnp.sum(lp[2:3, :] * lp[3:4, :], axis=1, keepdims=True)
    return jnp.exp(a) - jnp.exp(b) + lam_init


def _scores_t(q, kb, mask):
    s = _dot_nt(kb, q)
    return s if mask is None else jnp.where(mask, s, MASK_VALUE)


def _softmax_step_t(q, kb, vb, m_ref, l_ref, acc_ref, mask):
    _softmax_update_t(_scores_t(q, kb, mask), vb, m_ref, l_ref, acc_ref)


def _softmax_update_t(s, vb, m_ref, l_ref, acc_ref, cols=slice(None)):
    m_old = m_ref[:, cols]
    m_new = jnp.maximum(m_old, jnp.max(s, axis=0, keepdims=True))
    alpha = jnp.exp2(m_old - m_new)
    p = jnp.exp2(s - m_new)
    l_ref[:, cols] = alpha * l_ref[:, cols] + jnp.sum(p, axis=0, keepdims=True)
    acc_ref[:, cols] = alpha * acc_ref[:, cols] + _dot_tn(vb, p)
    m_ref[:, cols] = m_new


def _diff_finish_t(lam, sub_ref, l0_ref, a0_ref, l1_ref, a1_ref):
    o_t = a0_ref[...] / l0_ref[...] - lam * (a1_ref[...] / l1_ref[...])
    o = o_t.T
    return o * lax.rsqrt(jnp.mean(o * o, axis=-1, keepdims=True) + EPS) * sub_ref[...]


def _init_softmax_state(m_ref, l_ref, acc_ref):
    m_ref[...] = jnp.full_like(m_ref, MASK_VALUE)
    l_ref[...] = jnp.zeros_like(l_ref)
    acc_ref[...] = jnp.zeros_like(acc_ref)


def _flash_kernel(q0_ref, q1_ref, k_ref, v_ref, lp_ref, sub_ref, mix_ref, out_ref, *state, seq, tq, tk, lam_init):
    del mix_ref
    qi = pl.program_id(1)
    groups = tq // tk
    chains = [state[6 * g:6 * g + 6] for g in range(groups)]
    km, vm = k_ref[seq:seq + N_META, :], v_ref[seq:seq + N_META, :]
    queries = []
    for g, (m0, l0, a0, m1, l1, a1) in enumerate(chains):
        q0 = q0_ref[g * tk:(g + 1) * tk, :]
        q1 = q1_ref[g * tk:(g + 1) * tk, :]
        queries.append((q0, q1))
        _init_softmax_state(m0, l0, a0)
        _init_softmax_state(m1, l1, a1)
        _softmax_step_t(q0, km, vm, m0, l0, a0, None)
        _softmax_step_t(q1, km, vm, m1, l1, a1, None)

    ki = lax.broadcasted_iota(jnp.int32, (tk, tk), 0)
    qj = lax.broadcasted_iota(jnp.int32, (tk, tk), 1)
    causal = ki <= qj

    def step(j, first_group, diagonal):
        rows = pl.ds(pl.multiple_of(j * tk, tk), tk)
        kb, vb = k_ref[rows, :], v_ref[rows, :]
        scores = []
        for g in range(first_group, groups):
            mask = causal if (diagonal and g == first_group) else None
            scores.append((_scores_t(queries[g][0], kb, mask), _scores_t(queries[g][1], kb, mask)))
        for g, (s0, s1) in zip(range(first_group, groups), scores):
            m0, l0, a0, m1, l1, a1 = chains[g]
            _softmax_update_t(s0, vb, m0, l0, a0)
            _softmax_update_t(s1, vb, m1, l1, a1)

    def body(j, carry):
        step(j, 0, False)
        return carry

    lax.fori_loop(0, qi * groups, body, 0)
    for g in range(groups):
        step(qi * groups + g, g, True)

    lam = _lambda_value(lp_ref, lam_init)
    for g, (m0, l0, a0, m1, l1, a1) in enumerate(chains):
        out_ref[g * tk:(g + 1) * tk, :] = _diff_finish_t(lam, sub_ref, l0, a0, l1, a1).astype(out_ref.dtype)


def _flash(q0, q1, kb, vb, lam_p, subln, mix, seq, lam_init):
    width = q0.shape[1]
    heads = width // DIFF_DV
    col0 = (mix.shape[1] - width) // DIFF_DV
    tq = min(FLASH_Q_TILE, seq)
    tk = min(FLASH_K_TILE, tq)
    kv_rows = seq + N_META
    qspec = pl.BlockSpec((tq, DIFF_DV), lambda h, i: (i, h))
    kspec = pl.BlockSpec((kv_rows, DIFF_DV), lambda h, i: (0, h))
    stat = pltpu.VMEM((1, tk), F32)
    acc = pltpu.VMEM((DIFF_DV, tk), F32)
    return pl.pallas_call(
        functools.partial(_flash_kernel, seq=seq, tq=tq, tk=tk, lam_init=lam_init),
        grid=(heads, seq // tq),
        in_specs=[qspec, qspec, kspec, kspec,
                  pl.BlockSpec(lam_p.shape, lambda h, i: (0, 0)), pl.BlockSpec((1, DIFF_DV), lambda h, i: (0, 0)),
                  pl.BlockSpec(memory_space=pl.ANY)],
        out_specs=pl.BlockSpec((tq, DIFF_DV), lambda h, i: (i, col0 + h)),
        out_shape=jax.ShapeDtypeStruct(mix.shape, mix.dtype),
        input_output_aliases={6: 0},
        scratch_shapes=[stat, stat, acc, stat, stat, acc] * (tq // tk),
        compiler_params=_params("parallel", "arbitrary"),
        name="diff_attn_prompt",
    )(q0, q1, kb, vb, lam_p, subln, mix)


def _meta_attn_kernel(q0_ref, q1_ref, k_ref, v_ref, lp_ref, sub_ref, out_ref,
                      m0, l0, a0, m1, l1, a1, *, lam_init):
    _init_softmax_state(m0, l0, a0)
    _init_softmax_state(m1, l1, a1)
    t = q0_ref.shape[0]
    ki = lax.broadcasted_iota(jnp.int32, (t, t), 0)
    qj = lax.broadcasted_iota(jnp.int32, (t, t), 1)
    mask = ki <= qj
    _softmax_step_t(q0_ref[...], k_ref[...], v_ref[...], m0, l0, a0, mask)
    _softmax_step_t(q1_ref[...], k_ref[...], v_ref[...], m1, l1, a1, mask)
    out_ref[...] = _diff_finish_t(_lambda_value(lp_ref, lam_init), sub_ref, l0, a0, l1, a1)


def _meta_attn(q0, q1, kb, vb, lam_p, subln, lam_init):
    t, width = q0.shape
    heads = width // DIFF_DV
    spec = pl.BlockSpec((t, DIFF_DV), lambda h: (0, h))
    stat = pltpu.VMEM((1, t), F32)
    acc = pltpu.VMEM((DIFF_DV, t), F32)
    return pl.pallas_call(
        functools.partial(_meta_attn_kernel, lam_init=lam_init),
        grid=(heads,),
        in_specs=[spec, spec, spec, spec,
                  pl.BlockSpec(lam_p.shape, lambda h: (0, 0)), pl.BlockSpec((1, DIFF_DV), lambda h: (0, 0))],
        out_specs=spec,
        out_shape=jax.ShapeDtypeStruct((t, width), F32),
        scratch_shapes=[stat, stat, acc, stat, stat, acc],
        compiler_params=_params("parallel"),
        name="diff_attn_meta",
    )(q0, q1, kb, vb, lam_p, subln)


def _sample_attn_kernel(pt_ref, q_ref, *refs, pages, heads, new_valid, q_len, lam_init):
    k_refs = refs[:pages]
    v_refs = refs[pages:2 * pages]
    kn_ref, vn_ref, lp_ref, sub_ref, out_ref, kcat, vcat, m_s, l_s, acc_s = refs[2 * pages:]
    j = pl.program_id(1)
    page = k_refs[0].shape[1]
    keys = PAGES_PER_UPDATE * page
    rows_q = q_ref.shape[0]
    group = rows_q // heads

    @pl.when(j == 0)
    def _():
        m_s[...] = jnp.full_like(m_s, MASK_VALUE)
        l_s[...] = jnp.zeros_like(l_s)
        acc_s[...] = jnp.zeros_like(acc_s)

    q = q_ref[...]

    def update(k_t, weighted_values, mask):
        s = jnp.dot(q, k_t, preferred_element_type=F32)
        if mask is not None:
            s = jnp.where(mask, s, MASK_VALUE)
        m_old = m_s[...]
        m_new = jnp.maximum(m_old, jnp.max(s, axis=1, keepdims=True))
        alpha = jnp.exp2(m_old - m_new)
        p = jnp.exp2(s - m_new)
        l_s[...] = alpha * l_s[...] + jnp.sum(p, axis=1, keepdims=True)
        acc_s[...] = alpha * acc_s[...] + weighted_values(p)
        m_s[...] = m_new

    def cache_values(p):
        own_head = (lax.broadcasted_iota(jnp.int32, (heads, rows_q), 0)
                    == lax.broadcasted_iota(jnp.int32, (heads, rows_q), 1) // group)
        p_t = p.T
        expanded = jnp.where(own_head[None], p_t[:, None, :], 0.0).reshape(keys * heads, rows_q)
        return _dot_tn(expanded, vcat[...])

    def new_values(p):
        return jnp.concatenate(
            [jnp.dot(p[h * group:(h + 1) * group, :].astype(BF16), vn_ref[:, h * DIFF_DV:(h + 1) * DIFF_DV],
                     preferred_element_type=F32) for h in range(heads)], axis=0)

    for first in range(0, pages, PAGES_PER_UPDATE):
        for i in range(PAGES_PER_UPDATE):
            kcat[:, i * page:(i + 1) * page] = k_refs[first + i][...].astype(BF16)
            vcat[i * page * heads:(i + 1) * page * heads, :] = v_refs[first + i][...].astype(BF16)
        update(kcat[...], cache_values, None)

    @pl.when(j == pl.num_programs(1) - 1)
    def _():
        n_new = kn_ref.shape[1]
        r = lax.broadcasted_iota(jnp.int32, (rows_q, n_new), 0)
        u = lax.broadcasted_iota(jnp.int32, (rows_q, n_new), 1)
        update(kn_ref[...], new_values, (u <= r % q_len) & (u < new_valid))
        on = acc_s[...] / l_s[...]
        lam = _lambda_value(lp_ref, lam_init)
        d = on - lam * pltpu.roll(on, rows_q - q_len, 0)
        out_ref[...] = d * lax.rsqrt(jnp.mean(d * d, axis=-1, keepdims=True) + EPS) * sub_ref[...]


def _sample_attn(page_table, qbd, cache_kt, cache_v, k_new_t, v_new, lam_p, subln, q_len, lam_init):
    nb, rows_q, width = qbd.shape
    heads = width // DIFF_DV
    page = cache_kt.shape[2]
    n_pages = page_table.shape[1]
    pages = PAGES_PER_STEP
    k_spec = lambda i: pl.BlockSpec((None, width, page), lambda b, j, pt, i=i: (pt[b, j * pages + i], 0, 0))
    v_spec = lambda i: pl.BlockSpec((None, page * heads, DIFF_DV),
                                    lambda b, j, pt, i=i: (pt[b, j * pages + i], 0, 0))
    per_b = lambda shape: pl.BlockSpec((None,) + shape, lambda b, j, pt: (b, 0, 0))
    grid_spec = pltpu.PrefetchScalarGridSpec(
        num_scalar_prefetch=1,
        grid=(nb, n_pages // pages),
        in_specs=[per_b((rows_q, width))] + [k_spec(i) for i in range(pages)] + [v_spec(i) for i in range(pages)]
                 + [per_b(k_new_t.shape[1:]), per_b(v_new.shape[1:]),
                    pl.BlockSpec(lam_p.shape, lambda b, j, pt: (0, 0)),
                    pl.BlockSpec((1, DIFF_DV), lambda b, j, pt: (0, 0))],
        out_specs=pl.BlockSpec((None, rows_q, DIFF_DV), lambda b, j, pt: (b, 0, 0)),
        scratch_shapes=[pltpu.VMEM((width, PAGES_PER_UPDATE * page), BF16),
                        pltpu.VMEM((heads * PAGES_PER_UPDATE * page, DIFF_DV), BF16),
                        pltpu.VMEM((rows_q, 1), F32), pltpu.VMEM((rows_q, 1), F32),
                        pltpu.VMEM((rows_q, DIFF_DV), F32)],
    )
    return pl.pallas_call(
        functools.partial(_sample_attn_kernel, pages=pages, heads=heads, new_valid=q_len, q_len=q_len,
                          lam_init=lam_init),
        grid_spec=grid_spec,
        out_shape=jax.ShapeDtypeStruct((nb, rows_q, DIFF_DV), F32),
        compiler_params=_params("parallel", "arbitrary"),
        name="diff_attn_sample",
    )(page_table, qbd, *([cache_kt] * pages), *([cache_v] * pages), k_new_t, v_new, lam_p, subln)


def _rope_tables(pos):
    half = DIFF_DH // 2
    inv = ROPE_THETA ** (-jnp.arange(half, dtype=F32) / half)
    ang = jnp.asarray(pos, F32)[:, None] * inv[None, :]
    cos, sin = jnp.cos(ang), jnp.sin(ang)
    cos = jnp.tile(cos, (1, LANES // half))
    sin = jnp.tile(jnp.concatenate([-sin, sin], axis=1), (1, LANES // DIFF_DH))
    return cos, sin


def kernel(x_prompt, x_sample, cache_k, cache_v, state_C, state_n, state_m, page_table, meta_tokens,
           norm_ffn1, ffn1_w_gate, ffn1_w_up, ffn1_w_down, norm_mix, w_in, mlstm_b_i, mlstm_b_f,
           mlstm_norm, diff_lambda_qk, diff_subln, w_out, norm_ffn2, ffn2_w_gate, ffn2_w_up,
           ffn2_w_down, norm_final):
    batch, seq, d_model = x_prompt.shape
    assert batch == 1 and state_C.shape[0] == 1
    dec_b, dec_t, _ = x_sample.shape
    heads_m = mlstm_b_i.shape[1]
    dk, dv = state_C.shape[3], state_C.shape[4]
    qw, vw = heads_m * dk, heads_m * dv
    dw = w_out.shape[1] - vw
    heads_d = dw // DIFF_DV
    assert vw == 2 * qw and dw == vw, "column blocks of the projection are addressed in units of the q width"
    n_s = dec_b * dec_t
    rows = seq + N_META + n_s
    past = page_table.shape[1] * cache_k.shape[2]
    lam_init = 0.8 - 0.6 * math.exp(-0.0)

    def ffn_weights(wg, wu, wd):
        return wg, wu, wd.astype(BF16)

    h = jnp.concatenate([x_prompt[0], meta_tokens.astype(F32), x_sample.reshape(n_s, d_model)], axis=0)

    h = _ffn(h, norm_ffn1[0], *ffn_weights(ffn1_w_gate[0], ffn1_w_up[0], ffn1_w_down[0]))

    wt = w_in[0].T
    c_gate = 2 * qw + 2 * vw
    wt_gate = jnp.pad(wt[c_gate:c_gate + 2 * heads_m], ((0, LANES - 2 * heads_m), (0, 0)))
    u = _rmsnorm(h, norm_mix[0], BF16, 256)
    tm = _row_tile(rows, 1088)
    z = _mm_nt(u, wt, _row_tile(rows, 2176), _col_tile(c_gate), c_gate)
    zg = _mm_nt(u, wt_gate, _row_tile(rows, 2176), LANES)
    pos = np.concatenate([N_META + np.arange(seq), np.arange(N_META), past + np.tile(np.arange(dec_t), dec_b)])
    cos, sin = _rope_tables(pos)
    wt_d = wt[c_gate + 2 * heads_m:].astype(BF16)
    tn_d = _col_tile(dw)
    q0, q1 = _proj_rope(u, wt_d, 0, dw, cos, sin, "q", tm, tn_d)
    k_rot, kb = _proj_rope(u, wt_d, dw, dw, cos, sin, "k", tm, tn_d)
    v_all, vb = _proj_rope(u, wt_d, 2 * dw, dw, cos, sin, "v", tm, tn_d)
    gate_bias = jnp.pad(jnp.concatenate([mlstm_b_i[0], mlstm_b_f[0]]), (0, LANES - 2 * heads_m)).reshape(1, LANES)
    gates = _gates(zg, gate_bias.astype(F32), heads_m)
    log_i, log_f = gates[:, :heads_m], gates[:, heads_m:2 * heads_m]

    def chunk_rows(a):
        a = a.T.reshape(heads_m, seq // CHUNK, CHUNK)
        return jnp.pad(a, ((0, 0), (0, 0), (0, LANES - CHUNK)))

    lane_rows = lambda a: jnp.pad(a, ((0, 0),) * (a.ndim - 1) + ((0, LANES - a.shape[-1]),))
    norm_m = mlstm_norm[0].reshape(1, vw).astype(F32)
    mix, mo_meta, c_p, n_p, m_p = _mlstm_prompt(
        z, chunk_rows(log_i[:seq]), chunk_rows(log_f[:seq]),
        lane_rows(log_i[seq:seq + N_META].T), lane_rows(log_f[seq:seq + N_META].T), norm_m, seq, heads_m, dk, dv,
        vw + dw)

    t_pad = 8
    zs = jnp.pad(z[seq + N_META:].reshape(dec_b, dec_t, c_gate), ((0, 0), (0, t_pad - dec_t), (0, 0)))
    gate_s = lambda a: lane_rows(a[seq + N_META:].reshape(dec_b, dec_t, heads_m).transpose(0, 2, 1))
    m0 = jnp.broadcast_to(state_m[0][:, :, None], (dec_b, heads_m, LANES))
    mo_s, c_s, n_s_new, m_s = _mlstm_sample(zs, gate_s(log_i), gate_s(log_f), state_C[0], state_n[0], m0, norm_m,
                                             heads_m, dk, dv, dec_t)

    sub =(diff_subln[0].astype(F32) * (1.0 - lam_init)).reshape(1, DIFF_DV)
    lam_p = diff_lambda_qk[0].astype(F32)
    mix = _flash(q0, q1, kb, vb, lam_p, sub, mix, seq, lam_init)

    pad128 = lambda a: jnp.pad(a, ((0, LANES - a.shape[0]), (0, 0)))
    meta = slice(seq, seq + N_META)
    do_meta = _meta_attn(pad128(q0[meta]), pad128(q1[meta]), pad128(kb[meta]), pad128(vb[meta]), lam_p, sub,
                         lam_init)[:N_META]

    qs = (q0[seq + N_META:] + q1[seq + N_META:]).reshape(dec_b, 1, dec_t, dw)
    rows_q = heads_d * 2 * dec_t
    qbd = jnp.broadcast_to(qs, (dec_b, heads_d * 2, dec_t, dw)).reshape(dec_b, rows_q, dw)
    r_id = np.arange(rows_q)[:, None] // dec_t
    c_id = np.arange(dw)[None, :] // DIFF_DH
    qbd = jnp.where(jnp.asarray(r_id == c_id), qbd, jnp.zeros_like(qbd))
    new_rows = lambda a: jnp.pad(a[seq + N_META:].reshape(dec_b, dec_t, dw), ((0, 0), (0, LANES - dec_t), (0, 0)))
    n_phys, page = cache_k.shape[1], cache_k.shape[2]
    cache_kt = jnp.transpose(cache_k[0], (0, 2, 3, 4, 1)).reshape(n_phys, dw, page)
    do_s = _sample_attn(page_table, qbd, cache_kt, cache_v[0].reshape(n_phys, page * heads_d, DIFF_DV),
                        new_rows(kb).transpose(0, 2, 1), new_rows(vb), lam_p, sub, dec_t, lam_init)
    do_s = do_s.reshape(dec_b, heads_d, 2, dec_t, DIFF_DV)[:, :, 0].transpose(0, 2, 1, 3).reshape(n_s, dw)

    tail = jnp.concatenate([
        jnp.concatenate([mo_meta, mo_s[:, :dec_t].reshape(n_s, vw).astype(BF16)], axis=0),
        jnp.concatenate([do_meta.astype(BF16), do_s.astype(BF16)], axis=0)], axis=1)
    mix = lax.dynamic_update_slice(mix, tail, (seq, 0))
    h = _mm_resid(mix, w_out[0], h, 1.0, _row_tile(rows, 1088), 512)

    h = _ffn(h, norm_ffn2[0], *ffn_weights(ffn2_w_gate[0], ffn2_w_up[0], ffn2_w_down[0]))

    y_prompt = _rmsnorm(h, norm_final, F32, 256, 0, seq // 256).reshape(1, seq, d_model)
    y_sample = _rmsnorm(h, norm_final, F32, N_META, (seq + N_META) // N_META, n_s // N_META)
    y_sample = y_sample.reshape(dec_b, dec_t, d_model)

    order = lambda a: jnp.concatenate([a[meta], a[:seq]], axis=0)
    k_prompt = order(k_rot).reshape(1, 1, seq + N_META, heads_d, 2, DIFF_DH)
    v_prompt = order(v_all).reshape(1, 1, seq + N_META, heads_d, DIFF_DV)
    k_sample = k_rot[seq + N_META:].reshape(1, dec_b, dec_t, heads_d, 2, DIFF_DH)
    v_sample = v_all[seq + N_META:].reshape(1, dec_b, dec_t, heads_d, DIFF_DV)
    return (y_prompt, y_sample, k_prompt, v_prompt,
            c_p[None, None], n_p[None, None], m_p[None, None, :, 0],
            k_sample, v_sample, c_s[None], n_s_new[None], m_s[None, :, :, 0])
```

```python
import functools
import math

import numpy as np
import jax
import jax.numpy as jnp
from jax import lax
from jax.experimental import pallas as pl
from jax.experimental.pallas import tpu as pltpu

F32 = jnp.float32
BF16 = jnp.bfloat16

N_META = 16
CHUNK = 64
ROPE_THETA = 10000.0
GATE_CAP = 15.0
M_INIT = -1e30
EPS = 1e-6
DIFF_DV = 128
DIFF_DH = DIFF_DV // 2
DIFF_SCALE = DIFF_DH ** -0.5
Q_SCALE_LOG2 = DIFF_SCALE * math.log2(math.e)
PAGES_PER_STEP = 8
PAGES_PER_UPDATE = 4
FLASH_Q_TILE = 2048
FLASH_K_TILE = 512
MASK_VALUE = -1e30

LANES = 128
V7X_VMEM_BYTES = 64 * 2 ** 20
VMEM_LIMIT = V7X_VMEM_BYTES - 8 * 2 ** 20


def _params(*sem):
    return pltpu.CompilerParams(dimension_semantics=sem, vmem_limit_bytes=VMEM_LIMIT)


def _round_up(x, m):
    return (x + m - 1) // m * m


def _row_tile(rows, target):
    n = pl.cdiv(rows, target)
    return _round_up(pl.cdiv(rows, n), 16)


def _col_tile(n):
    return 512 if n % 512 == 0 else 256


def _dot(a, b):
    return jnp.dot(a.astype(BF16), b.astype(BF16), preferred_element_type=F32)


def _dot_nt(a, b):
    return lax.dot_general(a.astype(BF16), b.astype(BF16), (((1,), (1,)), ((), ())),
                           preferred_element_type=F32)


def _dot_tn(a, b):
    return lax.dot_general(a.astype(BF16), b.astype(BF16), (((0,), (0,)), ((), ())),
                           preferred_element_type=F32)


def _rmsnorm_kernel(x_ref, g_ref, o_ref):
    x = x_ref[...]
    y = x * lax.rsqrt(jnp.mean(x * x, axis=-1, keepdims=True) + EPS)
    o_ref[...] = (y * g_ref[...]).astype(o_ref.dtype)


def _rmsnorm(x, g, out_dtype, row_block, first_block=0, n_blocks=None):
    rows, d = x.shape
    if n_blocks is None:
        n_blocks, out_rows = pl.cdiv(rows, row_block), rows
    else:
        out_rows = n_blocks * row_block
    return pl.pallas_call(
        _rmsnorm_kernel,
        grid=(n_blocks,),
        in_specs=[pl.BlockSpec((row_block, d), lambda i: (i + first_block, 0)),
                  pl.BlockSpec((1, d), lambda i: (0, 0))],
        out_specs=pl.BlockSpec((row_block, d), lambda i: (i, 0)),
        out_shape=jax.ShapeDtypeStruct((out_rows, d), out_dtype),
        compiler_params=_params("parallel"),
        name="rmsnorm",
    )(x, g.reshape(1, d).astype(F32))


def _gateup_kernel(x_ref, wg_ref, wu_ref, o_ref):
    x = x_ref[...]
    g = jnp.dot(x, wg_ref[...].astype(BF16), preferred_element_type=F32)
    u = jnp.dot(x, wu_ref[...].astype(BF16), preferred_element_type=F32)
    o_ref[...] = (g * jax.nn.sigmoid(g) * u).astype(o_ref.dtype)


def _gateup(x, wg, wu, tm, tn):
    rows, d = x.shape
    n = wg.shape[1]
    return pl.pallas_call(
        _gateup_kernel,
        grid=(pl.cdiv(rows, tm), n // tn),
        in_specs=[pl.BlockSpec((tm, d), lambda i, j: (i, 0), pipeline_mode=pl.Buffered(1)),
                  pl.BlockSpec((d, tn), lambda i, j: (0, j)),
                  pl.BlockSpec((d, tn), lambda i, j: (0, j))],
        out_specs=pl.BlockSpec((tm, tn), lambda i, j: (i, j)),
        out_shape=jax.ShapeDtypeStruct((rows, n), BF16),
        compiler_params=_params("parallel", "arbitrary"),
        name="ffn_gate_up",
    )(x, wg, wu)


def _mm_resid_kernel(a_ref, w_ref, r_ref, o_ref, *, scale):
    acc = jnp.dot(a_ref[...], w_ref[...].astype(BF16), preferred_element_type=F32)
    o_ref[...] = r_ref[...] + scale * acc


def _mm_resid(a, w, resid, scale, tm, tn):
    rows, k = a.shape
    n = w.shape[1]
    return pl.pallas_call(
        functools.partial(_mm_resid_kernel, scale=scale),
        grid=(pl.cdiv(rows, tm), n // tn),
        in_specs=[pl.BlockSpec((tm, k), lambda i, j: (i, 0)),
                  pl.BlockSpec((k, tn), lambda i, j: (0, j)),
                  pl.BlockSpec((tm, tn), lambda i, j: (i, j))],
        out_specs=pl.BlockSpec((tm, tn), lambda i, j: (i, j)),
        out_shape=jax.ShapeDtypeStruct((rows, n), F32),
        compiler_params=_params("parallel", "arbitrary"),
        name="matmul_residual",
    )(a, w, resid)


def _mm_nt_kernel(a_ref, wt_ref, o_ref):
    o_ref[...] = _dot_nt(a_ref[...], wt_ref[...])


def _mm_nt(a, wt, tm, tn, n=None):
    rows, k = a.shape
    n = wt.shape[0] if n is None else n
    return pl.pallas_call(
        _mm_nt_kernel,
        grid=(pl.cdiv(rows, tm), n // tn),
        in_specs=[pl.BlockSpec((tm, k), lambda i, j: (i, 0), pipeline_mode=pl.Buffered(1)),
                  pl.BlockSpec((tn, k), lambda i, j: (j, 0))],
        out_specs=pl.BlockSpec((tm, tn), lambda i, j: (i, j)),
        out_shape=jax.ShapeDtypeStruct((rows, n), F32),
        compiler_params=_params("parallel", "arbitrary"),
        name="matmul_nt",
    )(a, wt)


def _ffn(h, g, wg, wu, wd):
    rows = h.shape[0]
    xn = _rmsnorm(h, g, BF16, 256)
    hid = _gateup(xn, wg, wu, _row_tile(rows, 2176), _col_tile(wg.shape[1]))
    return _mm_resid(hid, wd, h, 0.5, _row_tile(rows, 544), 512)


def _gate_kernel(z_ref, b_ref, o_ref, *, heads):
    z = z_ref[...] + b_ref[...]
    c = GATE_CAP * jnp.tanh(z / GATE_CAP)
    log_sig = jnp.minimum(c, 0.0) - jnp.log1p(jnp.exp(-jnp.abs(c)))
    lane = lax.broadcasted_iota(jnp.int32, c.shape, 1)
    o_ref[...] = jnp.where(lane < heads, c, log_sig)


def _gates(z, bias, heads):
    rows, d = z.shape
    rb = 512
    return pl.pallas_call(
        functools.partial(_gate_kernel, heads=heads),
        grid=(pl.cdiv(rows, rb),),
        in_specs=[pl.BlockSpec((rb, d), lambda i: (i, 0)), pl.BlockSpec((1, d), lambda i: (0, 0))],
        out_specs=pl.BlockSpec((rb, d), lambda i: (i, 0)),
        out_shape=jax.ShapeDtypeStruct((rows, d), F32),
        compiler_params=_params("parallel"),
        name="mlstm_gates",
    )(z, bias)


def _proj_rope_kernel(a_ref, wt_ref, cos_ref, sin_ref, o0_ref, o1_ref, *, mode):
    z = _dot_nt(a_ref[...], wt_ref[...])
    if mode == "v":
        o0_ref[...] = z
        o1_ref[...] = z.astype(BF16)
        return
    cos = cos_ref[...]
    sin = sin_ref[...]
    lane = lax.broadcasted_iota(jnp.int32, cos.shape, 1)
    first_half = (lane % DIFF_DH) < (DIFF_DH // 2)
    map0 = lane < DIFF_DH
    for j in range(z.shape[1] // LANES):
        sl = slice(j * LANES, (j + 1) * LANES)
        x = z[:, sl]
        partner = jnp.where(first_half, pltpu.roll(x, LANES - DIFF_DH // 2, 1), pltpu.roll(x, DIFF_DH // 2, 1))
        r = x * cos + partner * sin
        if mode == "q":
            r = r * Q_SCALE_LOG2
            o0_ref[:, sl] = jnp.where(map0, r, 0.0).astype(BF16)
            o1_ref[:, sl] = jnp.where(map0, 0.0, r).astype(BF16)
        else:
            o0_ref[:, sl] = r
            o1_ref[:, sl] = r.astype(BF16)


def _proj_rope(a, wt, row0, width, cos, sin, mode, tm, tn):
    rows, k = a.shape
    off = row0 // tn
    tspec = pl.BlockSpec((tm, LANES), lambda i, j: (i, 0))
    ospec = pl.BlockSpec((tm, tn), lambda i, j: (i, j))
    return pl.pallas_call(
        functools.partial(_proj_rope_kernel, mode=mode),
        grid=(pl.cdiv(rows, tm), width // tn),
        in_specs=[pl.BlockSpec((tm, k), lambda i, j: (i, 0), pipeline_mode=pl.Buffered(1)),
                  pl.BlockSpec((tn, k), lambda i, j: (j + off, 0)), tspec, tspec],
        out_specs=[ospec, ospec],
        out_shape=[jax.ShapeDtypeStruct((rows, width), BF16 if mode == "q" else F32),
                   jax.ShapeDtypeStruct((rows, width), BF16)],
        compiler_params=_params("parallel", "arbitrary"),
        name="proj_rope_" + mode,
    )(a, wt, cos, sin)


def _lane_cumsum(x):
    lane = lax.broadcasted_iota(jnp.int32, x.shape, 1)
    k = 1
    while k < x.shape[1]:
        x = x + jnp.where(lane >= k, pltpu.roll(x, k, 1), 0.0)
        k *= 2
    return x


def _mlstm_heads(load, store, c_ref, n_ref, m_ref, heads, valid):
    ins = [load(h) for h in range(heads)]
    t, dk = ins[0][0].shape
    ri = lax.broadcasted_iota(jnp.int32, (t, t), 0)
    ci = lax.broadcasted_iota(jnp.int32, (t, t), 1)
    eye = ri == ci
    lane = lax.broadcasted_iota(jnp.int32, (1, t), 1)

    stage1 = []
    for h, (q, k, v, o, li_row, b_row, norm_row) in enumerate(ins):
        k = k * (dk ** -0.5)
        c_state = c_ref[h]
        stage1.append((k, c_state, _dot_nt(q, k), _dot(q, c_state)))

    stage2 = []
    for h, (q, k_raw, v, o, li_row, b_row, norm_row) in enumerate(ins):
        k, c_state, qk, qc = stage1[h]
        m_prev = m_ref[h:h + 1, 0:1]
        b_mat = jnp.broadcast_to(b_row, (t, t))
        b_col = jnp.sum(jnp.where(eye, b_mat, 0.0), axis=1, keepdims=True)
        log_d = jnp.where(ci <= ri, b_col - b_mat + li_row, -jnp.inf)
        m_inter = m_prev + b_col
        m_t = jnp.maximum(m_inter, jnp.max(log_d, axis=1, keepdims=True))
        w_inter = jnp.exp(m_inter - m_t)
        s = qk * jnp.exp(log_d - m_t)
        m_new = m_t[valid - 1:valid, :]
        decay = jnp.exp(m_inter[valid - 1:valid, :] - m_new)
        wk_row = jnp.where(lane < valid, jnp.exp(b_row[:, valid - 1:valid] - b_row + li_row - m_new), 0.0)
        wk_col = jnp.sum(jnp.where(eye, jnp.broadcast_to(wk_row, (t, t)), 0.0), axis=1, keepdims=True)
        stage2.append((s, k * wk_col, w_inter, m_t, m_new, decay))

    stage3 = []
    for h, (q, k_raw, v, o, li_row, b_row, norm_row) in enumerate(ins):
        s, kw = stage2[h][:2]
        stage3.append((_dot(s, v), _dot_tn(kw, v)))

    for h, (q, k_raw, v, o, li_row, b_row, norm_row) in enumerate(ins):
        k, c_state, qk, qc = stage1[h]
        s, kw, w_inter, m_t, m_new, decay = stage2[h]
        sv, kv = stage3[h]
        n_row = n_ref[h:h + 1, :]
        num = w_inter * qc + sv
        den = w_inter * jnp.sum(q * n_row, axis=1, keepdims=True) + jnp.sum(s, axis=1, keepdims=True)
        y = num / jnp.maximum(jnp.abs(den), jnp.exp(-m_t))
        c_ref[h] = decay * c_state + kv
        n_ref[h:h + 1, :] = decay * n_row + jnp.sum(kw, axis=0, keepdims=True)
        m_ref[h:h + 1, :] = jnp.broadcast_to(m_new, (1, m_ref.shape[1]))
        yn = y * lax.rsqrt(jnp.mean(y * y, axis=-1, keepdims=True) + EPS) * norm_row
        store(h, jax.nn.sigmoid(o) * yn)


def _mlstm_prompt_kernel(q_ref, k_ref, v_ref, o_ref, li_ref, lf_ref,
                         qm_ref, km_ref, vm_ref, om_ref, lim_ref, lfm_ref, norm_ref,
                         out_ref, outm_ref, c_ref, n_ref, m_ref, *, heads, dk, dv, chunks):
    g = pl.program_id(0)

    @pl.when(g == 0)
    def _():
        c_ref[...] = jnp.zeros_like(c_ref)
        n_ref[...] = jnp.zeros_like(n_ref)
        m_ref[...] = jnp.full_like(m_ref, M_INIT)
        bm = _lane_cumsum(lfm_ref[...])

        def load_meta(h):
            return (qm_ref[:, h * dk:(h + 1) * dk], km_ref[:, h * dk:(h + 1) * dk],
                    vm_ref[:, h * dv:(h + 1) * dv], om_ref[:, h * dv:(h + 1) * dv],
                    lim_ref[h:h + 1, 0:N_META], bm[h:h + 1, 0:N_META], norm_ref[:, h * dv:(h + 1) * dv])

        def store_meta(h, y):
            outm_ref[:, h * dv:(h + 1) * dv] = y.astype(outm_ref.dtype)

        _mlstm_heads(load_meta, store_meta, c_ref, n_ref, m_ref, heads, N_META)

    def body(c, carry):
        r0 = pl.multiple_of(c * CHUNK, CHUNK)
        rows = pl.ds(r0, CHUNK)
        gc = g * chunks + c

        def load(h):
            return (q_ref[rows, h * dk:(h + 1) * dk], k_ref[rows, h * dk:(h + 1) * dk],
                    v_ref[rows, h * dv:(h + 1) * dv], o_ref[rows, h * dv:(h + 1) * dv],
                    li_ref[h, pl.ds(gc, 1), :][:, 0:CHUNK], _lane_cumsum(lf_ref[h, pl.ds(gc, 1), :])[:, 0:CHUNK],
                    norm_ref[:, h * dv:(h + 1) * dv])

        def store(h, y):
            out_ref[rows, h * dv:(h + 1) * dv] = y.astype(out_ref.dtype)

        _mlstm_heads(load, store, c_ref, n_ref, m_ref, heads, CHUNK)
        return carry

    lax.fori_loop(0, chunks, body, 0)


def _mlstm_prompt(z, li, lf, li_meta, lf_meta, norm, seq, heads, dk, dv, mix_width):
    chunks = 4
    rb = chunks * CHUNK
    qw, vw = heads * dk, heads * dv
    meta_blk = seq // N_META
    spec = lambda w, c: pl.BlockSpec((rb, w), lambda g, c=c: (g, c))
    mspec = lambda w, c: pl.BlockSpec((N_META, w), lambda g, c=c: (meta_blk, c))
    gspec = pl.BlockSpec((heads, seq // CHUNK, LANES), lambda g: (0, 0, 0))
    const2 = lambda shape: pl.BlockSpec(shape, lambda g: (0,) * len(shape))
    return pl.pallas_call(
        functools.partial(_mlstm_prompt_kernel, heads=heads, dk=dk, dv=dv, chunks=chunks),
        grid=(seq // rb,),
        in_specs=[spec(qw, 0), spec(qw, 1), spec(vw, 1), spec(vw, 2), gspec, gspec,
                  mspec(qw, 0), mspec(qw, 1), mspec(vw, 1), mspec(vw, 2),
                  const2((heads, LANES)), const2((heads, LANES)), const2((1, vw))],
        out_specs=[pl.BlockSpec((rb, vw), lambda g: (g, 0)), const2((N_META, vw)),
                   const2((heads, dk, dv)), const2((heads, dk)), const2((heads, LANES))],
        out_shape=[jax.ShapeDtypeStruct((z.shape[0], mix_width), BF16), jax.ShapeDtypeStruct((N_META, vw), BF16),
                   jax.ShapeDtypeStruct((heads, dk, dv), F32), jax.ShapeDtypeStruct((heads, dk), F32),
                   jax.ShapeDtypeStruct((heads, LANES), F32)],
        compiler_params=_params("arbitrary"),
        name="mlstm_prompt",
    )(z, z, z, z, li, lf, z, z, z, z, li_meta, lf_meta, norm)


def _mlstm_sample_kernel(q_ref, k_ref, v_ref, o_ref, li_ref, lf_ref, c0_ref, n0_ref, m0_ref, norm_ref,
                         out_ref, c_ref, n_ref, m_ref, *, heads, dk, dv, valid):
    c_ref[...] = c0_ref[...]
    n_ref[...] = n0_ref[...]
    m_ref[...] = m0_ref[...]
    t = q_ref.shape[0]
    b_all = _lane_cumsum(lf_ref[...])

    def load(h):
        return (q_ref[:, h * dk:(h + 1) * dk], k_ref[:, h * dk:(h + 1) * dk],
                v_ref[:, h * dv:(h + 1) * dv], o_ref[:, h * dv:(h + 1) * dv],
                li_ref[h:h + 1, 0:t], b_all[h:h + 1, 0:t], norm_ref[:, h * dv:(h + 1) * dv])

    def store(h, y):
        out_ref[:, h * dv:(h + 1) * dv] = y

    _mlstm_heads(load, store, c_ref, n_ref, m_ref, heads, valid)


def _mlstm_sample(zs, li, lf, c0, n0, m0, norm, heads, dk, dv, valid):
    nb, t, _ = zs.shape
    qw, vw = heads * dk, heads * dv
    spec = lambda w, c: pl.BlockSpec((None, t, w), lambda b, c=c: (b, 0, c))
    per_b = lambda *shape: pl.BlockSpec((None,) + shape, lambda b: (b,) + (0,) * len(shape))
    return pl.pallas_call(
        functools.partial(_mlstm_sample_kernel, heads=heads, dk=dk, dv=dv, valid=valid),
        grid=(nb,),
        in_specs=[spec(qw, 0), spec(qw, 1), spec(vw, 1), spec(vw, 2), per_b(heads, LANES), per_b(heads, LANES),
                  per_b(heads, dk, dv), per_b(heads, dk), per_b(heads, LANES),
                  pl.BlockSpec((1, vw), lambda b: (0, 0))],
        out_specs=[per_b(t, vw), per_b(heads, dk, dv), per_b(heads, dk), per_b(heads, LANES)],
        out_shape=[jax.ShapeDtypeStruct((nb, t, vw), F32), jax.ShapeDtypeStruct((nb, heads, dk, dv), F32),
                   jax.ShapeDtypeStruct((nb, heads, dk), F32), jax.ShapeDtypeStruct((nb, heads, LANES), F32)],
        compiler_params=_params("parallel"),
        name="mlstm_sample",
    )(zs, zs, zs, zs, li, lf, c0, n0, m0, norm)


def _lambda_value(lp_ref, lam_init):
    lp = lp_ref[...]
    a = jnp.sum(lp[0:1, :] * lp[1:2, :], axis=1, keepdims=True)
    b = jnp.sum(lp[2:3, :] * lp[3:4, :], axis=1, keepdims=True)
    return jnp.exp(a) - jnp.exp(b) + lam_init


def _scores_t(q, kb, mask):
    s = _dot_nt(kb, q)
    return s if mask is None else jnp.where(mask, s, MASK_VALUE)


def _softmax_step_t(q, kb, vb, m_ref, l_ref, acc_ref, mask):
    _softmax_update_t(_scores_t(q, kb, mask), vb, m_ref, l_ref, acc_ref)


def _softmax_update_t(s, vb, m_ref, l_ref, acc_ref, cols=slice(None)):
    m_old = m_ref[:, cols]
    m_new = jnp.maximum(m_old, jnp.max(s, axis=0, keepdims=True))
    alpha = jnp.exp2(m_old - m_new)
    p = jnp.exp2(s - m_new)
    l_ref[:, cols] = alpha * l_ref[:, cols] + jnp.sum(p, axis=0, keepdims=True)
    acc_ref[:, cols] = alpha * acc_ref[:, cols] + _dot_tn(vb, p)
    m_ref[:, cols] = m_new


def _diff_finish_t(lam, sub_ref, l0_ref, a0_ref, l1_ref, a1_ref):
    o_t = a0_ref[...] / l0_ref[...] - lam * (a1_ref[...] / l1_ref[...])
    o = o_t.T
    return o * lax.rsqrt(jnp.mean(o * o, axis=-1, keepdims=True) + EPS) * sub_ref[...]


def _init_softmax_state(m_ref, l_ref, acc_ref):
    m_ref[...] = jnp.full_like(m_ref, MASK_VALUE)
    l_ref[...] = jnp.zeros_like(l_ref)
    acc_ref[...] = jnp.zeros_like(acc_ref)


def _flash_kernel(q0_ref, q1_ref, k_ref, v_ref, lp_ref, sub_ref, mix_ref, out_ref, *state, seq, tq, tk, lam_init):
    del mix_ref
    qi = pl.program_id(1)
    groups = tq // tk
    chains = [state[6 * g:6 * g + 6] for g in range(groups)]
    km, vm = k_ref[seq:seq + N_META, :], v_ref[seq:seq + N_META, :]
    queries = []
    for g, (m0, l0, a0, m1, l1, a1) in enumerate(chains):
        q0 = q0_ref[g * tk:(g + 1) * tk, :]
        q1 = q1_ref[g * tk:(g + 1) * tk, :]
        queries.append((q0, q1))
        _init_softmax_state(m0, l0, a0)
        _init_softmax_state(m1, l1, a1)
        _softmax_step_t(q0, km, vm, m0, l0, a0, None)
        _softmax_step_t(q1, km, vm, m1, l1, a1, None)

    ki = lax.broadcasted_iota(jnp.int32, (tk, tk), 0)
    qj = lax.broadcasted_iota(jnp.int32, (tk, tk), 1)
    causal = ki <= qj

    def step(j, first_group, diagonal):
        rows = pl.ds(pl.multiple_of(j * tk, tk), tk)
        kb, vb = k_ref[rows, :], v_ref[rows, :]
        scores = []
        for g in range(first_group, groups):
            mask = causal if (diagonal and g == first_group) else None
            scores.append((_scores_t(queries[g][0], kb, mask), _scores_t(queries[g][1], kb, mask)))
        for g, (s0, s1) in zip(range(first_group, groups), scores):
            m0, l0, a0, m1, l1, a1 = chains[g]
            _softmax_update_t(s0, vb, m0, l0, a0)
            _softmax_update_t(s1, vb, m1, l1, a1)

    def body(j, carry):
        step(j, 0, False)
        return carry

    lax.fori_loop(0, qi * groups, body, 0)
    for g in range(groups):
        step(qi * groups + g, g, True)

    lam = _lambda_value(lp_ref, lam_init)
    for g, (m0, l0, a0, m1, l1, a1) in enumerate(chains):
        out_ref[g * tk:(g + 1) * tk, :] = _diff_finish_t(lam, sub_ref, l0, a0, l1, a1).astype(out_ref.dtype)


def _flash(q0, q1, kb, vb, lam_p, subln, mix, seq, lam_init):
    width = q0.shape[1]
    heads = width // DIFF_DV
    col0 = (mix.shape[1] - width) // DIFF_DV
    tq = min(FLASH_Q_TILE, seq)
    tk = min(FLASH_K_TILE, tq)
    kv_rows = seq + N_META
    qspec = pl.BlockSpec((tq, DIFF_DV), lambda h, i: (i, h))
    kspec = pl.BlockSpec((kv_rows, DIFF_DV), lambda h, i: (0, h))
    stat = pltpu.VMEM((1, tk), F32)
    acc = pltpu.VMEM((DIFF_DV, tk), F32)
    return pl.pallas_call(
        functools.partial(_flash_kernel, seq=seq, tq=tq, tk=tk, lam_init=lam_init),
        grid=(heads, seq // tq),
        in_specs=[qspec, qspec, kspec, kspec,
                  pl.BlockSpec(lam_p.shape, lambda h, i: (0, 0)), pl.BlockSpec((1, DIFF_DV), lambda h, i: (0, 0)),
                  pl.BlockSpec(memory_space=pl.ANY)],
        out_specs=pl.BlockSpec((tq, DIFF_DV), lambda h, i: (i, col0 + h)),
        out_shape=jax.ShapeDtypeStruct(mix.shape, mix.dtype),
        input_output_aliases={6: 0},
        scratch_shapes=[stat, stat, acc, stat, stat, acc] * (tq // tk),
        compiler_params=_params("parallel", "arbitrary"),
        name="diff_attn_prompt",
    )(q0, q1, kb, vb, lam_p, subln, mix)


def _meta_attn_kernel(q0_ref, q1_ref, k_ref, v_ref, lp_ref, sub_ref, out_ref,
                      m0, l0, a0, m1, l1, a1, *, lam_init):
    _init_softmax_state(m0, l0, a0)
    _init_softmax_state(m1, l1, a1)
    t = q0_ref.shape[0]
    ki = lax.broadcasted_iota(jnp.int32, (t, t), 0)
    qj = lax.broadcasted_iota(jnp.int32, (t, t), 1)
    mask = ki <= qj
    _softmax_step_t(q0_ref[...], k_ref[...], v_ref[...], m0, l0, a0, mask)
    _softmax_step_t(q1_ref[...], k_ref[...], v_ref[...], m1, l1, a1, mask)
    out_ref[...] = _diff_finish_t(_lambda_value(lp_ref, lam_init), sub_ref, l0, a0, l1, a1)


def _meta_attn(q0, q1, kb, vb, lam_p, subln, lam_init):
    t, width = q0.shape
    heads = width // DIFF_DV
    spec = pl.BlockSpec((t, DIFF_DV), lambda h: (0, h))
    stat = pltpu.VMEM((1, t), F32)
    acc = pltpu.VMEM((DIFF_DV, t), F32)
    return pl.pallas_call(
        functools.partial(_meta_attn_kernel, lam_init=lam_init),
        grid=(heads,),
        in_specs=[spec, spec, spec, spec,
                  pl.BlockSpec(lam_p.shape, lambda h: (0, 0)), pl.BlockSpec((1, DIFF_DV), lambda h: (0, 0))],
        out_specs=spec,
        out_shape=jax.ShapeDtypeStruct((t, width), F32),
        scratch_shapes=[stat, stat, acc, stat, stat, acc],
        compiler_params=_params("parallel"),
        name="diff_attn_meta",
    )(q0, q1, kb, vb, lam_p, subln)


def _sample_attn_kernel(pt_ref, q_ref, *refs, pages, heads, new_valid, q_len, lam_init):
    k_refs = refs[:pages]
    v_refs = refs[pages:2 * pages]
    kn_ref, vn_ref, lp_ref, sub_ref, out_ref, kcat, vcat, m_s, l_s, acc_s = refs[2 * pages:]
    j = pl.program_id(1)
    page = k_refs[0].shape[1]
    keys = PAGES_PER_UPDATE * page
    rows_q = q_ref.shape[0]
    group = rows_q // heads

    @pl.when(j == 0)
    def _():
        m_s[...] = jnp.full_like(m_s, MASK_VALUE)
        l_s[...] = jnp.zeros_like(l_s)
        acc_s[...] = jnp.zeros_like(acc_s)

    q = q_ref[...]

    def update(k_t, weighted_values, mask):
        s = jnp.dot(q, k_t, preferred_element_type=F32)
        if mask is not None:
            s = jnp.where(mask, s, MASK_VALUE)
        m_old = m_s[...]
        m_new = jnp.maximum(m_old, jnp.max(s, axis=1, keepdims=True))
        alpha = jnp.exp2(m_old - m_new)
        p = jnp.exp2(s - m_new)
        l_s[...] = alpha * l_s[...] + jnp.sum(p, axis=1, keepdims=True)
        acc_s[...] = alpha * acc_s[...] + weighted_values(p)
        m_s[...] = m_new

    def cache_values(p):
        own_head = (lax.broadcasted_iota(jnp.int32, (heads, rows_q), 0)
                    == lax.broadcasted_iota(jnp.int32, (heads, rows_q), 1) // group)
        p_t = p.T
        expanded = jnp.where(own_head[None], p_t[:, None, :], 0.0).reshape(keys * heads, rows_q)
        return _dot_tn(expanded, vcat[...])

    def new_values(p):
        return jnp.concatenate(
            [jnp.dot(p[h * group:(h + 1) * group, :].astype(BF16), vn_ref[:, h * DIFF_DV:(h + 1) * DIFF_DV],
                     preferred_element_type=F32) for h in range(heads)], axis=0)

    for first in range(0, pages, PAGES_PER_UPDATE):
        for i in range(PAGES_PER_UPDATE):
            kcat[:, i * page:(i + 1) * page] = k_refs[first + i][...].astype(BF16)
            vcat[i * page * heads:(i + 1) * page * heads, :] = v_refs[first + i][...].astype(BF16)
        update(kcat[...], cache_values, None)

    @pl.when(j == pl.num_programs(1) - 1)
    def _():
        n_new = kn_ref.shape[1]
        r = lax.broadcasted_iota(jnp.int32, (rows_q, n_new), 0)
        u = lax.broadcasted_iota(jnp.int32, (rows_q, n_new), 1)
        update(kn_ref[...], new_values, (u <= r % q_len) & (u < new_valid))
        on = acc_s[...] / l_s[...]
        lam = _lambda_value(lp_ref, lam_init)
        d = on - lam * pltpu.roll(on, rows_q - q_len, 0)
        out_ref[...] = d * lax.rsqrt(jnp.mean(d * d, axis=-1, keepdims=True) + EPS) * sub_ref[...]


def _sample_attn(page_table, qbd, cache_kt, cache_v, k_new_t, v_new, lam_p, subln, q_len, lam_init):
    nb, rows_q, width = qbd.shape
    heads = width // DIFF_DV
    page = cache_kt.shape[2]
    n_pages = page_table.shape[1]
    pages = PAGES_PER_STEP
    k_spec = lambda i: pl.BlockSpec((None, width, page), lambda b, j, pt, i=i: (pt[b, j * pages + i], 0, 0))
    v_spec = lambda i: pl.BlockSpec((None, page * heads, DIFF_DV),
                                    lambda b, j, pt, i=i: (pt[b, j * pages + i], 0, 0))
    per_b = lambda shape: pl.BlockSpec((None,) + shape, lambda b, j, pt: (b, 0, 0))
    grid_spec = pltpu.PrefetchScalarGridSpec(
        num_scalar_prefetch=1,
        grid=(nb, n_pages // pages),
        in_specs=[per_b((rows_q, width))] + [k_spec(i) for i in range(pages)] + [v_spec(i) for i in range(pages)]
                 + [per_b(k_new_t.shape[1:]), per_b(v_new.shape[1:]),
                    pl.BlockSpec(lam_p.shape, lambda b, j, pt: (0, 0)),
                    pl.BlockSpec((1, DIFF_DV), lambda b, j, pt: (0, 0))],
        out_specs=pl.BlockSpec((None, rows_q, DIFF_DV), lambda b, j, pt: (b, 0, 0)),
        scratch_shapes=[pltpu.VMEM((width, PAGES_PER_UPDATE * page), BF16),
                        pltpu.VMEM((heads * PAGES_PER_UPDATE * page, DIFF_DV), BF16),
                        pltpu.VMEM((rows_q, 1), F32), pltpu.VMEM((rows_q, 1), F32),
                        pltpu.VMEM((rows_q, DIFF_DV), F32)],
    )
    return pl.pallas_call(
        functools.partial(_sample_attn_kernel, pages=pages, heads=heads, new_valid=q_len, q_len=q_len,
                          lam_init=lam_init),
        grid_spec=grid_spec,
        out_shape=jax.ShapeDtypeStruct((nb, rows_q, DIFF_DV), F32),
        compiler_params=_params("parallel", "arbitrary"),
        name="diff_attn_sample",
    )(page_table, qbd, *([cache_kt] * pages), *([cache_v] * pages), k_new_t, v_new, lam_p, subln)


def _rope_tables(pos):
    half = DIFF_DH // 2
    inv = ROPE_THETA ** (-jnp.arange(half, dtype=F32) / half)
    ang = jnp.asarray(pos, F32)[:, None] * inv[None, :]
    cos, sin = jnp.cos(ang), jnp.sin(ang)
    cos = jnp.tile(cos, (1, LANES // half))
    sin = jnp.tile(jnp.concatenate([-sin, sin], axis=1), (1, LANES // DIFF_DH))
    return cos, sin


def kernel(x_prompt, x_sample, cache_k, cache_v, state_C, state_n, state_m, page_table, meta_tokens,
           norm_ffn1, ffn1_w_gate, ffn1_w_up, ffn1_w_down, norm_mix, w_in, mlstm_b_i, mlstm_b_f,
           mlstm_norm, diff_lambda_qk, diff_subln, w_out, norm_ffn2, ffn2_w_gate, ffn2_w_up,
           ffn2_w_down, norm_final):
    batch, seq, d_model = x_prompt.shape
    assert batch == 1 and state_C.shape[0] == 1
    dec_b, dec_t, _ = x_sample.shape
    heads_m = mlstm_b_i.shape[1]
    dk, dv = state_C.shape[3], state_C.shape[4]
    qw, vw = heads_m * dk, heads_m * dv
    dw = w_out.shape[1] - vw
    heads_d = dw // DIFF_DV
    assert vw == 2 * qw and dw == vw, "column blocks of the projection are addressed in units of the q width"
    n_s = dec_b * dec_t
    rows = seq + N_META + n_s
    past = page_table.shape[1] * cache_k.shape[2]
    lam_init = 0.8 - 0.6 * math.exp(-0.0)

    def ffn_weights(wg, wu, wd):
        return wg, wu, wd.astype(BF16)

    h = jnp.concatenate([x_prompt[0], meta_tokens.astype(F32), x_sample.reshape(n_s, d_model)], axis=0)

    h = _ffn(h, norm_ffn1[0], *ffn_weights(ffn1_w_gate[0], ffn1_w_up[0], ffn1_w_down[0]))

    wt = w_in[0].T
    c_gate = 2 * qw + 2 * vw
    wt_gate = jnp.pad(wt[c_gate:c_gate + 2 * heads_m], ((0, LANES - 2 * heads_m), (0, 0)))
    u = _rmsnorm(h, norm_mix[0], BF16, 256)
    tm = _row_tile(rows, 2176)
    z = _mm_nt(u, wt, tm, _col_tile(c_gate), c_gate)
    zg = _mm_nt(u, wt_gate, tm, LANES)
    pos = np.concatenate([N_META + np.arange(seq), np.arange(N_META), past + np.tile(np.arange(dec_t), dec_b)])
    cos, sin = _rope_tables(pos)
    wt_d = wt[c_gate + 2 * heads_m:].astype(BF16)
    tn_d = _col_tile(dw)
    q0, q1 = _proj_rope(u, wt_d, 0, dw, cos, sin, "q", tm, tn_d)
    k_rot, kb = _proj_rope(u, wt_d, dw, dw, cos, sin, "k", tm, tn_d)
    v_all, vb = _proj_rope(u, wt_d, 2 * dw, dw, cos, sin, "v", tm, tn_d)
    gate_bias = jnp.pad(jnp.concatenate([mlstm_b_i[0], mlstm_b_f[0]]), (0, LANES - 2 * heads_m)).reshape(1, LANES)
    gates = _gates(zg, gate_bias.astype(F32), heads_m)
    log_i, log_f = gates[:, :heads_m], gates[:, heads_m:2 * heads_m]

    def chunk_rows(a):
        a = a.T.reshape(heads_m, seq // CHUNK, CHUNK)
        return jnp.pad(a, ((0, 0), (0, 0), (0, LANES - CHUNK)))

    lane_rows = lambda a: jnp.pad(a, ((0, 0),) * (a.ndim - 1) + ((0, LANES - a.shape[-1]),))
    norm_m = mlstm_norm[0].reshape(1, vw).astype(F32)
    mix, mo_meta, c_p, n_p, m_p = _mlstm_prompt(
        z, chunk_rows(log_i[:seq]), chunk_rows(log_f[:seq]),
        lane_rows(log_i[seq:seq + N_META].T), lane_rows(log_f[seq:seq + N_META].T), norm_m, seq, heads_m, dk, dv,
        vw + dw)

    t_pad = 8
    zs = jnp.pad(z[seq + N_META:].reshape(dec_b, dec_t, c_gate), ((0, 0), (0, t_pad - dec_t), (0, 0)))
    gate_s = lambda a: lane_rows(a[seq + N_META:].reshape(dec_b, dec_t, heads_m).transpose(0, 2, 1))
    m0 = jnp.broadcast_to(state_m[0][:, :, None], (dec_b, heads_m, LANES))
    mo_s, c_s, n_s_new, m_s = _mlstm_sample(zs, gate_s(log_i), gate_s(log_f), state_C[0], state_n[0], m0, norm_m,
                                             heads_m, dk, dv, dec_t)

    sub =(diff_subln[0].astype(F32) * (1.0 - lam_init)).reshape(1, DIFF_DV)
    lam_p = diff_lambda_qk[0].astype(F32)
    mix = _flash(q0, q1, kb, vb, lam_p, sub, mix, seq, lam_init)

    pad128 = lambda a: jnp.pad(a, ((0, LANES - a.shape[0]), (0, 0)))
    meta = slice(seq, seq + N_META)
    do_meta = _meta_attn(pad128(q0[meta]), pad128(q1[meta]), pad128(kb[meta]), pad128(vb[meta]), lam_p, sub,
                         lam_init)[:N_META]

    qs = (q0[seq + N_META:] + q1[seq + N_META:]).reshape(dec_b, 1, dec_t, dw)
    rows_q = heads_d * 2 * dec_t
    qbd = jnp.broadcast_to(qs, (dec_b, heads_d * 2, dec_t, dw)).reshape(dec_b, rows_q, dw)
    r_id = np.arange(rows_q)[:, None] // dec_t
    c_id = np.arange(dw)[None, :] // DIFF_DH
    qbd = jnp.where(jnp.asarray(r_id == c_id), qbd, jnp.zeros_like(qbd))
    new_rows = lambda a: jnp.pad(a[seq + N_META:].reshape(dec_b, dec_t, dw), ((0, 0), (0, LANES - dec_t), (0, 0)))
    n_phys, page = cache_k.shape[1], cache_k.shape[2]
    cache_kt = jnp.transpose(cache_k[0], (0, 2, 3, 4, 1)).reshape(n_phys, dw, page)
    do_s = _sample_attn(page_table, qbd, cache_kt, cache_v[0].reshape(n_phys, page * heads_d, DIFF_DV),
                        new_rows(kb).transpose(0, 2, 1), new_rows(vb), lam_p, sub, dec_t, lam_init)
    do_s = do_s.reshape(dec_b, heads_d, 2, dec_t, DIFF_DV)[:, :, 0].transpose(0, 2, 1, 3).reshape(n_s, dw)

    tail = jnp.concatenate([
        jnp.concatenate([mo_meta, mo_s[:, :dec_t].reshape(n_s, vw).astype(BF16)], axis=0),
        jnp.concatenate([do_meta.astype(BF16), do_s.astype(BF16)], axis=0)], axis=1)
    mix = lax.dynamic_update_slice(mix, tail, (seq, 0))
    h = _mm_resid(mix, w_out[0], h, 1.0, _row_tile(rows, 1088), 512)

    h = _ffn(h, norm_ffn2[0], *ffn_weights(ffn2_w_gate[0], ffn2_w_up[0], ffn2_w_down[0]))

    y_prompt = _rmsnorm(h, norm_final, F32, 256, 0, seq // 256).reshape(1, seq, d_model)
    y_sample = _rmsnorm(h, norm_final, F32, N_META, (seq + N_META) // N_META, n_s // N_META)
    y_sample = y_sample.reshape(dec_b, dec_t, d_model)

    order = lambda a: jnp.concatenate([a[meta], a[:seq]], axis=0)
    k_prompt = order(k_rot).reshape(1, 1, seq + N_META, heads_d, 2, DIFF_DH)
    v_prompt = order(v_all).reshape(1, 1, seq + N_META, heads_d, DIFF_DV)
    k_sample = k_rot[seq + N_META:].reshape(1, dec_b, dec_t, heads_d, 2, DIFF_DH)
    v_sample = v_all[seq + N_META:].reshape(1, dec_b, dec_t, heads_d, DIFF_DV)
    return (y_prompt, y_sample, k_prompt, v_prompt,
            c_p[None, None], n_p[None, None], m_p[None, None, :, 0],
            k_sample, v_sample, c_s[None], n_s_new[None], m_s[None, :, :, 0])
```
